```python
import jax, jax.numpy as jnp
from jax import lax
import numpy as np

D_MODEL = 1024
BATCH = 4
SEQ = 8192
DEPTH = 1
DEC_BATCH = 128
DEC_SEQ = 1
PAST_LEN = 16384
PAGE_SIZE = 128

HEAD_DIM = 64
N_Q_HEADS = 8
N_KV_HEADS = 2
GQA_GROUP = N_Q_HEADS // N_KV_HEADS
ATTN_WIDTH = N_Q_HEADS * HEAD_DIM
KV_WIDTH = N_KV_HEADS * HEAD_DIM
WINDOW = 128
ROT_DIM = HEAD_DIM // 4
ROPE_THETA = 500000.0
N_REC_HEADS = 4
REC_DK = 128
REC_DV = 128
REC_WIDTH = N_REC_HEADS * REC_DK
REC_VWIDTH = N_REC_HEADS * REC_DV
REC_CHUNK = 64
MIX_WIDTH = ATTN_WIDTH + REC_VWIDTH
IN_SIZES = (ATTN_WIDTH, KV_WIDTH, KV_WIDTH, REC_WIDTH, REC_WIDTH, REC_VWIDTH, REC_VWIDTH)
IN_WIDTH = sum(IN_SIZES)
SPLIT_POINTS = tuple(int(s) for s in np.cumsum(IN_SIZES)[:-1])
D_FF = -(-8 * D_MODEL // (3 * 256)) * 256
EPS = 1e-6

kernel_name = 'hymba_swa_sink_hgrn2_decode_step'

F32 = jnp.float32


def _rmsnorm(x, g):
    xf = x.astype(F32)
    y = xf * lax.rsqrt(jnp.mean(xf * xf, axis=-1, keepdims=True) + EPS)
    return (y * g.astype(F32)).astype(x.dtype)


def _rope(x, pos):
    half = ROT_DIM // 2
    inv_freq = jnp.exp(-jnp.log(jnp.asarray(ROPE_THETA, F32)) * jnp.arange(half, dtype=F32) * (2.0 / ROT_DIM))
    ang = pos[:, None] * inv_freq[None, :]
    cos = jnp.cos(ang)[None, :, None, :]
    sin = jnp.sin(ang)[None, :, None, :]
    x1 = x[..., :half]
    x2 = x[..., half:ROT_DIM]
    return jnp.concatenate([x1 * cos - x2 * sin, x2 * cos + x1 * sin, x[..., ROT_DIM:]], axis=-1)


def _sink_probs(s, sink):
    m = jnp.maximum(jnp.max(s, axis=-1, keepdims=True), sink)
    p = jnp.exp(s - m)
    return p / (jnp.sum(p, axis=-1, keepdims=True) + jnp.exp(sink - m))


def _swa_prompt(q, k, v, sinks):
    B, T = q.shape[:2]
    nb = -(-T // WINDOW)
    pad = nb * WINDOW - T
    if pad:
        cfg = ((0, 0), (0, pad), (0, 0), (0, 0))
        q, k, v = jnp.pad(q, cfg), jnp.pad(k, cfg), jnp.pad(v, cfg)
    qb = q.reshape(B, nb, WINDOW, N_KV_HEADS, GQA_GROUP, HEAD_DIM)
    kb = k.reshape(B, nb, WINDOW, N_KV_HEADS, HEAD_DIM)
    vb = v.reshape(B, nb, WINDOW, N_KV_HEADS, HEAD_DIM)
    kk = jnp.concatenate([jnp.concatenate([jnp.zeros_like(kb[:, :1]), kb[:, :-1]], axis=1), kb], axis=2)
    vv = jnp.concatenate([jnp.concatenate([jnp.zeros_like(vb[:, :1]), vb[:, :-1]], axis=1), vb], axis=2)
    s = jnp.einsum('bnqhgd,bnkhd->bnhgqk', qb, kk) * (HEAD_DIM ** -0.5)
    qi = WINDOW + jnp.arange(WINDOW)
    ki = jnp.arange(2 * WINDOW)
    band = (ki[None, :] <= qi[:, None]) & (qi[:, None] - ki[None, :] < WINDOW)
    no_prev = (jnp.arange(nb)[:, None, None] == 0) & (ki < WINDOW)[None, None, :]
    valid = band[None] & ~no_prev
    s = jnp.where(valid[None, :, None, None], s, -jnp.inf)
    p = _sink_probs(s, sinks.reshape(1, 1, N_KV_HEADS, GQA_GROUP, 1, 1))
    o = jnp.einsum('bnhgqk,bnkhd->bnqhgd', p, vv)
    return o.reshape(B, nb * WINDOW, ATTN_WIDTH)[:, :T]


def _swa_sample(q, k, v, k_buf, v_buf, sinks, pos0):
    B, T = q.shape[:2]
    w_keep = k_buf.shape[1]
    kk = jnp.concatenate([k_buf.astype(F32), k], axis=1)
    vv = jnp.concatenate([v_buf.astype(F32), v], axis=1)
    qpos = pos0 + jnp.arange(T)
    kpos = pos0 - w_keep + jnp.arange(w_keep + T)
    valid = (kpos[None, :] <= qpos[:, None]) & (qpos[:, None] - kpos[None, :] < WINDOW)
    qg = q.reshape(B, T, N_KV_HEADS, GQA_GROUP, HEAD_DIM)
    s = jnp.einsum('bqhgd,bkhd->bhgqk', qg, kk) * (HEAD_DIM ** -0.5)
    s = jnp.where(valid[None, None, None], s, -jnp.inf)
    p = _sink_probs(s, sinks.reshape(1, N_KV_HEADS, GQA_GROUP, 1, 1))
    o = jnp.einsum('bhgqk,bkhd->bqhgd', p, vv).reshape(B, T, ATTN_WIDTH)
    return o, kk[:, -w_keep:], vv[:, -w_keep:]


def _hgrn2_scan(q, log_f, k, i, S0):
    B, T = q.shape[:2]
    C = min(REC_CHUNK, T)
    nc = -(-T // C)
    pad = nc * C - T
    if pad:
        cfg = ((0, 0), (0, pad), (0, 0), (0, 0))
        q, log_f, k, i = jnp.pad(q, cfg), jnp.pad(log_f, cfg), jnp.pad(k, cfg), jnp.pad(i, cfg)

    def to_chunks(a):
        return a.reshape(B, nc, C, *a.shape[2:]).swapaxes(0, 1)

    tri = jnp.tril(jnp.ones((C, C), dtype=bool))[None, :, :, None, None]

    def step(S, inp):
        qc, lc, kc, ic = inp
        b = jnp.cumsum(lc, axis=1)
        inter = jnp.einsum('bthk,bhkv->bthv', qc * jnp.exp(b), S)
        decay = jnp.exp(jnp.where(tri, b[:, :, None] - b[:, None, :], -jnp.inf))
        scores = jnp.einsum('btshk,bthk,bshk->btsh', decay, qc, kc)
        intra = jnp.einsum('btsh,bshv->bthv', scores, ic)
        b_last = b[:, -1]
        S_new = jnp.exp(b_last)[..., None] * S + jnp.einsum('bshk,bshv->bhkv', kc * jnp.exp(b_last[:, None] - b), ic)
        return S_new, inter + intra

    S_fin, o = lax.scan(step, S0, (to_chunks(q), to_chunks(log_f), to_chunks(k), to_chunks(i)))
    o = o.swapaxes(0, 1).reshape(B, nc * C, N_REC_HEADS, REC_DV)[:, :T]
    return o, S_fin


def _layer(x, pos0, w_in, w_out, w_gate, w_up, w_down, g_pre_mix, g_post_mix, g_pre_ffn, g_post_ffn,
           sinks, lb, rec_norm, k_buf, v_buf, S0):
    B, T, _ = x.shape
    h = _rmsnorm(x, g_pre_mix)
    aq, ak, av, rq, rf, ri, rg = jnp.split(h @ w_in, SPLIT_POINTS, axis=-1)
    pos = pos0 + jnp.arange(T, dtype=F32)
    q = _rope(aq.astype(F32).reshape(B, T, N_Q_HEADS, HEAD_DIM), pos)
    k = _rope(ak.astype(F32).reshape(B, T, N_KV_HEADS, HEAD_DIM), pos)
    v = av.astype(F32).reshape(B, T, N_KV_HEADS, HEAD_DIM)
    sinks = sinks.astype(F32)
    if k_buf is None:
        a = _swa_prompt(q, k, v, sinks)
        keep = min(WINDOW, T)
        k_new, v_new = k[:, -keep:], v[:, -keep:]
    else:
        a, k_new, v_new = _swa_sample(q, k, v, k_buf, v_buf, sinks, pos0)
    f = lb + (1.0 - lb) * jax.nn.sigmoid(rf.astype(F32))
    log_f = jnp.log(f).reshape(B, T, N_REC_HEADS, REC_DK)
    kr = (1.0 - f).reshape(B, T, N_REC_HEADS, REC_DK)
    qr = jax.nn.silu(rq.astype(F32)).reshape(B, T, N_REC_HEADS, REC_DK)
    ir = ri.astype(F32).reshape(B, T, N_REC_HEADS, REC_DV)
    o, S_new = _hgrn2_scan(qr, log_f, kr, ir, S0)
    o = o * lax.rsqrt(jnp.mean(o * o, axis=-1, keepdims=True) + EPS) * rec_norm.astype(F32)
    o = o.reshape(B, T, REC_VWIDTH) * jax.nn.silu(rg.astype(F32))
    mix = jnp.concatenate([a, o], axis=-1).astype(x.dtype) @ w_out
    x = x + _rmsnorm(mix, g_post_mix)
    h2 = _rmsnorm(x, g_pre_ffn)
    ffn = (jax.nn.silu(h2 @ w_gate) * (h2 @ w_up)) @ w_down
    x = x + _rmsnorm(ffn, g_post_ffn)
    return x, k_new, v_new, S_new


def setup_inputs(seed: int = 0) -> dict:
    key = jax.random.key(seed)
    ks = jax.random.split(key, 20)
    w_keep = min(WINDOW, PAST_LEN)
    nrm = jax.random.normal
    return {
        'x_prompt': nrm(ks[0], (BATCH, SEQ, D_MODEL), F32),
        'x_sample': nrm(ks[1], (DEC_BATCH, DEC_SEQ, D_MODEL), F32),
        'cache_k_win': nrm(ks[2], (DEPTH, DEC_BATCH, w_keep, N_KV_HEADS, HEAD_DIM), F32),
        'cache_v_win': nrm(ks[3], (DEPTH, DEC_BATCH, w_keep, N_KV_HEADS, HEAD_DIM), F32),
        'state_hgrn': 0.5 * nrm(ks[4], (DEPTH, DEC_BATCH, N_REC_HEADS, REC_DK, REC_DV), F32),
        'w_in': nrm(ks[5], (DEPTH, D_MODEL, IN_WIDTH), F32) * D_MODEL ** -0.5,
        'w_out': nrm(ks[6], (DEPTH, MIX_WIDTH, D_MODEL), F32) * MIX_WIDTH ** -0.5,
        'w_gate': nrm(ks[7], (DEPTH, D_MODEL, D_FF), F32) * D_MODEL ** -0.5,
        'w_up': nrm(ks[8], (DEPTH, D_MODEL, D_FF), F32) * D_MODEL ** -0.5,
        'w_down': nrm(ks[9], (DEPTH, D_FF, D_MODEL), F32) * D_FF ** -0.5,
        'norm_pre_mix': 1.0 + 0.02 * nrm(ks[10], (DEPTH, D_MODEL), F32),
        'norm_post_mix': 1.0 + 0.02 * nrm(ks[11], (DEPTH, D_MODEL), F32),
        'norm_pre_ffn': 1.0 + 0.02 * nrm(ks[12], (DEPTH, D_MODEL), F32),
        'norm_post_ffn': 1.0 + 0.02 * nrm(ks[13], (DEPTH, D_MODEL), F32),
        'attn_sinks': nrm(ks[14], (DEPTH, N_Q_HEADS), F32),
        'rec_lb': 0.5 * nrm(ks[15], (DEPTH + 1, REC_WIDTH), F32),
        'rec_out_norm': 1.0 + 0.02 * nrm(ks[16], (DEPTH, N_REC_HEADS, REC_DV), F32),
    }


def reference(x_prompt, x_sample, cache_k_win, cache_v_win, state_hgrn, w_in, w_out, w_gate, w_up, w_down,
              norm_pre_mix, norm_post_mix, norm_pre_ffn, norm_post_ffn, attn_sinks, rec_lb, rec_out_norm):
    lbs = jnp.cumsum(jax.nn.softmax(rec_lb.astype(F32), axis=0), axis=0)
    yp, ys = x_prompt, x_sample
    kp_l, vp_l, sp_l, ks_l, vs_l, ss_l = [], [], [], [], [], []
    for l in range(DEPTH):
        shared = (w_in[l], w_out[l], w_gate[l], w_up[l], w_down[l], norm_pre_mix[l], norm_post_mix[l],
                  norm_pre_ffn[l], norm_post_ffn[l], attn_sinks[l], lbs[l], rec_out_norm[l])
        S0 = jnp.zeros((yp.shape[0], N_REC_HEADS, REC_DK, REC_DV), F32)
        yp, kp, vp, sp = _layer(yp, 0, *shared, None, None, S0)
        ys, kn, vn, sn = _layer(ys, PAST_LEN, *shared, cache_k_win[l], cache_v_win[l], state_hgrn[l].astype(F32))
        kp_l.append(kp.astype(cache_k_win.dtype)); vp_l.append(vp.astype(cache_v_win.dtype)); sp_l.append(sp.astype(state_hgrn.dtype))
        ks_l.append(kn.astype(cache_k_win.dtype)); vs_l.append(vn.astype(cache_v_win.dtype)); ss_l.append(sn.astype(state_hgrn.dtype))
    new_k_win_prompt = jnp.stack(kp_l)
    new_v_win_prompt = jnp.stack(vp_l)
    new_state_prompt = jnp.stack(sp_l)
    new_k_win_sample = jnp.stack(ks_l)
    new_v_win_sample = jnp.stack(vs_l)
    new_state_sample = jnp.stack(ss_l)
    return (yp, ys, new_k_win_prompt, new_v_win_prompt, new_state_prompt, new_k_win_sample, new_v_win_sample, new_state_sample)
```

```python
import functools

import jax
import jax.numpy as jnp
import numpy as np
from jax import lax
from jax.experimental import pallas as pl
from jax.experimental.pallas import tpu as pltpu

F32 = jnp.float32
BF16 = jnp.bfloat16

PAST_LEN = 16384
WINDOW = 128
HEAD_DIM = 64
N_Q_HEADS = 8
N_KV_HEADS = 2
ROT_DIM = HEAD_DIM // 4
ROPE_THETA = 500000.0
N_REC_HEADS = 4
REC_DK = 128
EPS = 1e-6

ATTN_WIDTH = N_Q_HEADS * HEAD_DIM
KV_WIDTH = N_KV_HEADS * HEAD_DIM
REC_WIDTH = N_REC_HEADS * REC_DK
LANES = 128
NEG = -1e30

VMEM_LIMIT = 56 * 1024 * 1024


def _dot(a, b):
    return jnp.dot(a, b, preferred_element_type=F32)


def _dot_nt(a, b):
    return lax.dot_general(a, b, (((1,), (1,)), ((), ())), preferred_element_type=F32)


def _dot_tn(a, b):
    return lax.dot_general(a, b, (((0,), (0,)), ((), ())), preferred_element_type=F32)


def _sigmoid(x):
    return 1.0 / (1.0 + jnp.exp(-x))


def _rms(x, g):
    return x * lax.rsqrt(jnp.mean(x * x, axis=-1, keepdims=True) + EPS) * g


def _inproj_body(x_ref, g_ref, w_ref, cos_ref, sin_ref, lbp_ref,
                 q_ref, k_ref, v_ref, rq_ref, lf_ref, rk_ref, ri_ref, rg_ref):
    h = _rms(x_ref[...], g_ref[...]).astype(BF16)
    cos = cos_ref[...]
    sin = sin_ref[...]
    lane = lax.broadcasted_iota(jnp.int32, cos.shape, 1)
    first = (lane & (HEAD_DIM - 1)) < (ROT_DIM // 2)

    def rope(z):
        partner = jnp.where(first, pltpu.roll(z, LANES - ROT_DIM // 2, 1), pltpu.roll(z, ROT_DIM // 2, 1))
        return z * cos + partner * sin

    def proj(lo, width):
        return _dot(h, w_ref[:, lo:lo + width])

    scale = HEAD_DIM ** -0.5
    for j in range(ATTN_WIDTH // LANES):
        z = rope(proj(j * LANES, LANES))
        q_ref[:, j * LANES:(j + 1) * LANES] = (z * scale).astype(q_ref.dtype)
    off = ATTN_WIDTH
    k_ref[...] = rope(proj(off, KV_WIDTH))
    off += KV_WIDTH
    v_ref[...] = proj(off, KV_WIDTH)
    off += KV_WIDTH

    z = proj(off, REC_WIDTH)
    rq_ref[...] = (z * _sigmoid(z)).astype(rq_ref.dtype)
    off += REC_WIDTH

    r = lbp_ref[...]
    e = jnp.exp(r - jnp.max(r, axis=0, keepdims=True))
    lb = e[0:1, :] / jnp.sum(e, axis=0, keepdims=True)
    z = proj(off, REC_WIDTH)
    f = lb + (1.0 - lb) * _sigmoid(z)
    lf_ref[...] = jnp.log(f)
    rk_ref[...] = (1.0 - f).astype(rk_ref.dtype)
    off += REC_WIDTH

    ri_ref[...] = proj(off, REC_WIDTH).astype(ri_ref.dtype)
    off += REC_WIDTH
    z = proj(off, REC_WIDTH)
    rg_ref[...] = (z * _sigmoid(z)).astype(rg_ref.dtype)


def _inproj(x, g, w_bf, cos, sin, rec_lb, *, tm, act_dtype):
    n, d = x.shape
    t_tab = cos.shape[0]
    n_tab = t_tab // tm
    in_w = w_bf.shape[1]
    row = lambda i: (i, 0)
    const = lambda i: (0, 0)
    outs = [
        jax.ShapeDtypeStruct((n, ATTN_WIDTH), act_dtype),
        jax.ShapeDtypeStruct((n, KV_WIDTH), F32),
        jax.ShapeDtypeStruct((n, KV_WIDTH), F32),
        jax.ShapeDtypeStruct((n, REC_WIDTH), act_dtype),
        jax.ShapeDtypeStruct((n, REC_WIDTH), F32),
        jax.ShapeDtypeStruct((n, REC_WIDTH), act_dtype),
        jax.ShapeDtypeStruct((n, REC_WIDTH), act_dtype),
        jax.ShapeDtypeStruct((n, REC_WIDTH), act_dtype),
    ]
    return pl.pallas_call(
        _inproj_body,
        grid=(n // tm,),
        in_specs=[
            pl.BlockSpec((tm, d), row),
            pl.BlockSpec((1, d), const),
            pl.BlockSpec((d, in_w), const),
            pl.BlockSpec((tm, LANES), lambda i: (i % n_tab, 0)),
            pl.BlockSpec((tm, LANES), lambda i: (i % n_tab, 0)),
            pl.BlockSpec(rec_lb.shape, const),
        ],
        out_specs=[pl.BlockSpec((tm, o.shape[1]), row) for o in outs],
        out_shape=outs,
        compiler_params=pltpu.CompilerParams(
            dimension_semantics=("arbitrary",), vmem_limit_bytes=VMEM_LIMIT),
        name="inproj",
    )(x, g, w_bf, cos, sin, rec_lb)


def _head_variants(z):
    lane = lax.broadcasted_iota(jnp.int32, z.shape, 1)
    low = lane < HEAD_DIM
    z0 = jnp.where(low, z, 0.0)
    z1 = jnp.where(low, 0.0, z)
    return {
        (0, 0): z0.astype(BF16),
        (1, 1): z1.astype(BF16),
        (0, 1): pltpu.roll(z0, HEAD_DIM, 1).astype(BF16),
        (1, 0): pltpu.roll(z1, HEAD_DIM, 1).astype(BF16),
    }


def _swa_body(sink_ref, q_ref, kc_ref, vc_ref, kp_ref, vp_ref, a_ref, *, nblk):
    i = pl.program_id(1)
    w = WINDOW
    row = lax.broadcasted_iota(jnp.int32, (w, w), 0)
    col = lax.broadcasted_iota(jnp.int32, (w, w), 1)
    low = col < HEAD_DIM
    cur_ok = col <= row
    prev_band = col > row

    kvar = [_head_variants(kp_ref[...])]
    vvar = [_head_variants(vp_ref[...])]
    for j in range(nblk):
        kvar.append(_head_variants(kc_ref[j * w:(j + 1) * w, :]))
        vvar.append(_head_variants(vc_ref[j * w:(j + 1) * w, :]))

    group = N_Q_HEADS // N_KV_HEADS
    for j in range(nblk):
        prev_ok = prev_band if j > 0 else jnp.logical_and(prev_band, i > 0)
        for c in range(ATTN_WIDTH // LANES):
            qs = q_ref[j * w:(j + 1) * w, c * LANES:(c + 1) * LANES]
            halves = []
            for half in range(2):
                hq = 2 * c + half
                key = (hq // group, half)
                sink = sink_ref[hq]
                sp = jnp.where(prev_ok, _dot_nt(qs, kvar[j][key]), NEG)
                sc = jnp.where(cur_ok, _dot_nt(qs, kvar[j + 1][key]), NEG)
                m = jnp.maximum(jnp.max(sp, axis=-1, keepdims=True), jnp.max(sc, axis=-1, keepdims=True))
                m = jnp.maximum(m, sink)
                pp = jnp.exp(sp - m)
                pc = jnp.exp(sc - m)
                den = (jnp.sum(pp, axis=-1, keepdims=True) + jnp.sum(pc, axis=-1, keepdims=True)
                       + jnp.exp(sink - m))
                o = _dot(pp.astype(BF16), vvar[j][key]) + _dot(pc.astype(BF16), vvar[j + 1][key])
                halves.append(o / den)
            a_ref[j * w:(j + 1) * w, c * LANES:(c + 1) * LANES] = (
                jnp.where(low, halves[0], halves[1]).astype(a_ref.dtype))


def _swa_prompt(q, k, v, sinks, *, tq):
    b, t, _ = q.shape
    nblk = tq // WINDOW
    cur = lambda bi, i: (bi, i, 0)
    prev = lambda bi, i: (bi, jnp.maximum(i * nblk - 1, 0), 0)
    return pl.pallas_call(
        functools.partial(_swa_body, nblk=nblk),
        grid=(b, t // tq),
        in_specs=[
            pl.BlockSpec(memory_space=pltpu.SMEM),
            pl.BlockSpec((None, tq, ATTN_WIDTH), cur),
            pl.BlockSpec((None, tq, KV_WIDTH), cur),
            pl.BlockSpec((None, tq, KV_WIDTH), cur),
            pl.BlockSpec((None, WINDOW, KV_WIDTH), prev),
            pl.BlockSpec((None, WINDOW, KV_WIDTH), prev),
        ],
        out_specs=pl.BlockSpec((None, tq, ATTN_WIDTH), cur),
        out_shape=jax.ShapeDtypeStruct((b, t, ATTN_WIDTH), BF16),
        compiler_params=pltpu.CompilerParams(
            dimension_semantics=("arbitrary", "arbitrary"), vmem_limit_bytes=VMEM_LIMIT),
        name="swa_prompt",
    )(sinks, q, k, v, k, v)


def _split3(x):
    hi = x.astype(BF16)
    r1 = x - hi.astype(F32)
    mid = r1.astype(BF16)
    lo = (r1 - mid.astype(F32)).astype(BF16)
    return hi, mid, lo


def _level_table(c):
    t = np.arange(c)[:, None]
    s = np.arange(c)[None, :]
    x = t ^ s
    lvl = np.where(x > 0, np.floor(np.log2(np.maximum(x, 1))).astype(np.int32), 0)
    nlev = int(np.log2(c))
    return np.where(t > s, lvl, np.where(t == s, nlev, -1)).astype(np.int32)


def _hgrn_body(q_ref, lf_ref, k_ref, v_ref, g_ref, nrm_ref, lvl_ref, o_ref, sfin_ref, st_ref, b_ref,
               *, chunk, nchunk):
    i = pl.program_id(1)

    @pl.when(i == 0)
    def _():
        st_ref[...] = jnp.zeros_like(st_ref)

    c = chunk
    width = REC_WIDTH
    row = lax.broadcasted_iota(jnp.int32, (c, c), 0)
    col = lax.broadcasted_iota(jnp.int32, (c, c), 1)
    tri = (col <= row).astype(BF16)
    trow = lax.broadcasted_iota(jnp.int32, (c, width), 0)
    nlev = c.bit_length() - 1

    for ci in range(nchunk):
        sl = slice(ci * c, (ci + 1) * c)
        lf = lf_ref[sl, :]
        hi, mid, lo = _split3(lf)
        b = _dot(tri, hi) + _dot(tri, mid) + _dot(tri, lo)
        b_ref[...] = b
        q = q_ref[sl, :].astype(F32)
        k = k_ref[sl, :].astype(F32)
        v = v_ref[sl, :]
        blast = b[c - 1:c, :]
        q_in = (q * jnp.exp(b)).astype(BF16)
        k_out = (k * jnp.exp(blast - b)).astype(BF16)

        lev = []
        for l in range(nlev):
            m = 1 << l
            g2 = 2 * m
            if g2 >= 8:
                pieces = [jnp.broadcast_to(b_ref[gi * g2 + m - 1:gi * g2 + m, :], (g2, width))
                          for gi in range(c // g2)]
                bref = pieces[0] if len(pieces) == 1 else jnp.concatenate(pieces, axis=0)
            else:
                off = trow & (g2 - 1)
                bref = b
                for o in range(g2):
                    shift = o - (m - 1)
                    if shift != 0:
                        bref = jnp.where(off == o, pltpu.roll(b, shift % c, 0), bref)
            e = jnp.exp(-jnp.abs(b - bref))
            side = (trow & m) != 0
            lev.append((jnp.where(side, q, k) * e).astype(BF16))
        qb = q.astype(BF16)
        kb = k.astype(BF16)

        lvl = lvl_ref[...]
        for h in range(N_REC_HEADS):
            hs = slice(h * REC_DK, (h + 1) * REC_DK)
            sc = jnp.where(lvl == nlev, _dot_nt(qb[:, hs], kb[:, hs]), 0.0)
            for l in range(nlev):
                sc = jnp.where(lvl == l, _dot_nt(lev[l][:, hs], lev[l][:, hs]), sc)
            st = st_ref[h]
            oh = _dot_nt(q_in[:, hs], st.astype(BF16)) + _dot(sc.astype(BF16), v[:, hs])
            st_ref[h] = st * jnp.exp(blast[:, hs]) + _dot_tn(v[:, hs], k_out[:, hs])
            oh = oh * lax.rsqrt(jnp.mean(oh * oh, axis=-1, keepdims=True) + EPS) * nrm_ref[h:h + 1, :]
            o_ref[sl, hs] = (oh * g_ref[sl, hs].astype(F32)).astype(o_ref.dtype)

    @pl.when(i == pl.num_programs(1) - 1)
    def _():
        for h in range(N_REC_HEADS):
            sfin_ref[h] = st_ref[h].T


def _hgrn_prompt(q, lf, k, v, g, nrm, *, tc, chunk):
    b, t, w = q.shape
    cur = lambda bi, i: (bi, i, 0)
    const2 = lambda bi, i: (0, 0)
    lvl = jnp.asarray(_level_table(chunk))
    return pl.pallas_call(
        functools.partial(_hgrn_body, chunk=chunk, nchunk=tc // chunk),
        grid=(b, t // tc),
        in_specs=[
            pl.BlockSpec((None, tc, w), cur),
            pl.BlockSpec((None, tc, w), cur),
            pl.BlockSpec((None, tc, w), cur),
            pl.BlockSpec((None, tc, w), cur),
            pl.BlockSpec((None, tc, w), cur),
            pl.BlockSpec(nrm.shape, const2),
            pl.BlockSpec((chunk, chunk), const2),
        ],
        out_specs=[
            pl.BlockSpec((None, tc, w), cur),
            pl.BlockSpec((None, N_REC_HEADS, REC_DK, REC_DK), lambda bi, i: (bi, 0, 0, 0)),
        ],
        out_shape=[
            jax.ShapeDtypeStruct((b, t, w), BF16),
            jax.ShapeDtypeStruct((b, N_REC_HEADS, REC_DK, REC_DK), F32),
        ],
        scratch_shapes=[
            pltpu.VMEM((N_REC_HEADS, REC_DK, REC_DK), F32),
            pltpu.VMEM((chunk, w), F32),
        ],
        compiler_params=pltpu.CompilerParams(
            dimension_semantics=("arbitrary", "arbitrary"), vmem_limit_bytes=VMEM_LIMIT),
        name="hgrn_prompt",
    )(q, lf, k, v, g, nrm, lvl)


def _dec_attn_body(sink_ref, q_ref, kn_ref, vn_ref, ck_ref, cv_ref, a_ref, nk_ref, nv_ref,
                   qs_ref, os_ref, *, gb):
    group = N_Q_HEADS // N_KV_HEADS
    lane = lax.broadcasted_iota(jnp.int32, (gb, LANES), 1)
    low = lane < HEAD_DIM
    q = q_ref[...]
    for hq in range(N_Q_HEADS):
        z = q[:, (hq // 2) * LANES:(hq // 2 + 1) * LANES]
        z = jnp.where(low, z, 0.0) if hq % 2 == 0 else jnp.where(low, 0.0, z)
        if hq % 2 != hq // group:
            z = pltpu.roll(z, HEAD_DIM, 1)
        qs_ref[hq * gb:(hq + 1) * gb, :] = z

    hrow = lax.broadcasted_iota(jnp.int32, (N_Q_HEADS, 1), 0)
    sink = jnp.zeros((N_Q_HEADS, 1), F32)
    for hq in range(N_Q_HEADS):
        sink = jnp.where(hrow == hq, sink_ref[hq], sink)
    pad = jnp.zeros((8, LANES), BF16)
    w = ck_ref.shape[1]
    last = lax.broadcasted_iota(jnp.int32, (w, LANES), 0) == w - 1

    for j in range(gb):
        kk = jnp.where(last, kn_ref[j:j + 1, :], pltpu.roll(ck_ref[j], w - 1, 0))
        vv = jnp.where(last, vn_ref[j:j + 1, :], pltpu.roll(cv_ref[j], w - 1, 0))
        nk_ref[j] = kk
        nv_ref[j] = vv
        qp = qs_ref[pl.ds(j, N_Q_HEADS, stride=gb), :]
        qp = jnp.concatenate([qp.astype(BF16), pad], axis=0)
        s = _dot_nt(qp, kk.astype(BF16))[0:N_Q_HEADS, :]
        m = jnp.maximum(jnp.max(s, axis=-1, keepdims=True), sink)
        p = jnp.exp(s - m)
        den = jnp.sum(p, axis=-1, keepdims=True) + jnp.exp(sink - m)
        pb = jnp.concatenate([p.astype(BF16), pad], axis=0)
        o = _dot(pb, vv.astype(BF16))[0:N_Q_HEADS, :] / den
        os_ref[j * N_Q_HEADS:(j + 1) * N_Q_HEADS, :] = o

    for cidx in range(ATTN_WIDTH // LANES):
        parts = []
        for half in range(2):
            hq = 2 * cidx + half
            z = os_ref[pl.ds(hq, gb, stride=N_Q_HEADS), :]
            if half != hq // group:
                z = pltpu.roll(z, HEAD_DIM, 1)
            parts.append(z)
        a_ref[:, cidx * LANES:(cidx + 1) * LANES] = jnp.where(low, parts[0], parts[1]).astype(a_ref.dtype)


def _dec_attn(q, k_new, v_new, cache_k, cache_v, sinks, *, gb):
    nb = q.shape[0]
    w = cache_k.shape[1]
    row = lambda i: (i, 0)
    blk3 = lambda i: (i, 0, 0)
    return pl.pallas_call(
        functools.partial(_dec_attn_body, gb=gb),
        grid=(nb // gb,),
        in_specs=[
            pl.BlockSpec(memory_space=pltpu.SMEM),
            pl.BlockSpec((gb, ATTN_WIDTH), row),
            pl.BlockSpec((gb, KV_WIDTH), row),
            pl.BlockSpec((gb, KV_WIDTH), row),
            pl.BlockSpec((gb, w, KV_WIDTH), blk3),
            pl.BlockSpec((gb, w, KV_WIDTH), blk3),
        ],
        out_specs=[
            pl.BlockSpec((gb, ATTN_WIDTH), row),
            pl.BlockSpec((gb, w, KV_WIDTH), blk3),
            pl.BlockSpec((gb, w, KV_WIDTH), blk3),
        ],
        out_shape=[
            jax.ShapeDtypeStruct((nb, ATTN_WIDTH), BF16),
            jax.ShapeDtypeStruct(cache_k.shape, cache_k.dtype),
            jax.ShapeDtypeStruct(cache_v.shape, cache_v.dtype),
        ],
        scratch_shapes=[
            pltpu.VMEM((N_Q_HEADS * gb, LANES), F32),
            pltpu.VMEM((N_Q_HEADS * gb, LANES), F32),
        ],
        compiler_params=pltpu.CompilerParams(
            dimension_semantics=("arbitrary",), vmem_limit_bytes=VMEM_LIMIT),
        name="dec_attn",
    )(sinks, q, k_new, v_new, cache_k, cache_v)


def _dec_hgrn_body(q_ref, k_ref, i_ref, g_ref, nrm_ref, s_ref, o_ref, sn_ref, os_ref, *, gb):
    dk = REC_DK
    zpad = jnp.zeros((dk - gb, dk), F32)
    for h in range(N_REC_HEADS):
        hs = slice(h * dk, (h + 1) * dk)
        qt = jnp.concatenate([q_ref[:, hs], zpad], axis=0).T
        kt = jnp.concatenate([k_ref[:, hs], zpad], axis=0).T
        for j in range(gb):
            kcol = kt[:, j:j + 1]
            s_new = s_ref[j, h] * (1.0 - kcol) + kcol * i_ref[j:j + 1, hs]
            sn_ref[j, h] = s_new
            os_ref[j:j + 1, hs] = jnp.sum(s_new * qt[:, j:j + 1], axis=0, keepdims=True)
    for h in range(N_REC_HEADS):
        hs = slice(h * dk, (h + 1) * dk)
        oh = os_ref[:, hs]
        oh = oh * lax.rsqrt(jnp.mean(oh * oh, axis=-1, keepdims=True) + EPS) * nrm_ref[h:h + 1, :]
        o_ref[:, hs] = (oh * g_ref[:, hs]).astype(o_ref.dtype)


def _dec_hgrn(q, k, iv, g, nrm, state, *, gb):
    nb, w = q.shape
    row = lambda i: (i, 0)
    blk4 = lambda i: (i, 0, 0, 0)
    return pl.pallas_call(
        functools.partial(_dec_hgrn_body, gb=gb),
        grid=(nb // gb,),
        in_specs=[
            pl.BlockSpec((gb, w), row),
            pl.BlockSpec((gb, w), row),
            pl.BlockSpec((gb, w), row),
            pl.BlockSpec((gb, w), row),
            pl.BlockSpec(nrm.shape, lambda i: (0, 0)),
            pl.BlockSpec((gb,) + state.shape[1:], blk4),
        ],
        out_specs=[
            pl.BlockSpec((gb, w), row),
            pl.BlockSpec((gb,) + state.shape[1:], blk4),
        ],
        out_shape=[
            jax.ShapeDtypeStruct((nb, w), BF16),
            jax.ShapeDtypeStruct(state.shape, state.dtype),
        ],
        scratch_shapes=[pltpu.VMEM((gb, w), F32)],
        compiler_params=pltpu.CompilerParams(
            dimension_semantics=("arbitrary",), vmem_limit_bytes=VMEM_LIMIT),
        name="dec_hgrn",
    )(q, k, iv, g, nrm, state)


def _ffn_body(x_ref, a_ref, o_ref, wo_ref, wg_ref, wu_ref, wd_ref, g1_ref, g2_ref, g3_ref, y_ref):
    aw = a_ref.shape[1]
    mix = _dot(a_ref[...], wo_ref[0:aw, :]) + _dot(o_ref[...], wo_ref[aw:, :])
    x1 = x_ref[...] + _rms(mix, g1_ref[...])
    h2 = _rms(x1, g2_ref[...]).astype(BF16)
    gate = _dot(h2, wg_ref[...])
    up = _dot(h2, wu_ref[...])
    act = (gate * _sigmoid(gate) * up).astype(BF16)
    ffn = _dot(act, wd_ref[...])
    y_ref[...] = x1 + _rms(ffn, g3_ref[...])


def _ffn(x, a, o, wo, wg, wu, wd, g1, g2, g3, *, tm):
    n, d = x.shape
    row = lambda i: (i, 0)
    const = lambda i: (0, 0)
    resident = lambda arr: pl.BlockSpec(arr.shape, const, pipeline_mode=pl.Buffered(1))
    return pl.pallas_call(
        _ffn_body,
        grid=(n // tm,),
        in_specs=[
            pl.BlockSpec((tm, d), row),
            pl.BlockSpec((tm, a.shape[1]), row),
            pl.BlockSpec((tm, o.shape[1]), row),
            resident(wo), resident(wg), resident(wu), resident(wd),
            pl.BlockSpec((1, d), const), pl.BlockSpec((1, d), const), pl.BlockSpec((1, d), const),
        ],
        out_specs=pl.BlockSpec((tm, d), row),
        out_shape=jax.ShapeDtypeStruct((n, d), x.dtype),
        compiler_params=pltpu.CompilerParams(
            dimension_semantics=("arbitrary",), vmem_limit_bytes=VMEM_LIMIT),
        name="ffn",
    )(x, a, o, wo, wg, wu, wd, g1, g2, g3)


def _rope_tables(pos):
    half = ROT_DIM // 2
    inv_freq = jnp.exp(-jnp.log(jnp.asarray(ROPE_THETA, F32)) * jnp.arange(half, dtype=F32) * (2.0 / ROT_DIM))
    ang = pos[:, None] * inv_freq[None, :]
    cos = jnp.cos(ang)
    sin = jnp.sin(ang)
    ones = jnp.ones((pos.shape[0], HEAD_DIM - ROT_DIM), F32)
    cos_h = jnp.concatenate([cos, cos, ones], axis=-1)
    sin_h = jnp.concatenate([-sin, sin, 0.0 * ones], axis=-1)
    return jnp.concatenate([cos_h, cos_h], axis=-1), jnp.concatenate([sin_h, sin_h], axis=-1)


def _pick_tile(n, want):
    t = min(n, want)
    while n % t:
        t //= 2
    return t


def kernel(x_prompt, x_sample, cache_k_win, cache_v_win, state_hgrn, w_in, w_out, w_gate, w_up, w_down,
           norm_pre_mix, norm_post_mix, norm_pre_ffn, norm_post_ffn, attn_sinks, rec_lb, rec_out_norm):
    depth = w_in.shape[0]
    assert depth == 1, "single-layer step"
    bsz, seq, d = x_prompt.shape
    nb, dec_seq, _ = x_sample.shape
    assert dec_seq == 1 and seq % WINDOW == 0
    w_keep = cache_k_win.shape[2]
    assert w_keep == WINDOW

    w_in_b = w_in[0].astype(BF16)
    w_out_b = w_out[0].astype(BF16)
    w_gate_b = w_gate[0].astype(BF16)
    w_up_b = w_up[0].astype(BF16)
    w_down_b = w_down[0].astype(BF16)
    g_pre, g_post, g_pre_f, g_post_f = norm_pre_mix, norm_post_mix, norm_pre_ffn, norm_post_ffn
    sinks = attn_sinks[0].astype(F32)
    nrm = rec_out_norm[0].astype(F32)

    tm = _pick_tile(seq, 512)
    cos_p, sin_p = _rope_tables(jnp.arange(seq, dtype=F32))
    xp = x_prompt.reshape(bsz * seq, d)
    q, k, v, rq, lf, rk, ri, rg = _inproj(xp, g_pre, w_in_b, cos_p, sin_p, rec_lb, tm=tm, act_dtype=BF16)
    r3 = lambda z: z.reshape(bsz, seq, z.shape[-1])
    k3, v3 = r3(k), r3(v)
    a = _swa_prompt(r3(q), k3, v3, sinks, tq=tm)
    chunk = 128
    o, s_fin = _hgrn_prompt(r3(rq), r3(lf), r3(rk), r3(ri), r3(rg), nrm, tc=tm, chunk=chunk)
    tf = _pick_tile(bsz * seq, 256)
    yp = _ffn(xp, a.reshape(bsz * seq, -1), o.reshape(bsz * seq, -1), w_out_b, w_gate_b, w_up_b, w_down_b,
              g_post, g_pre_f, g_post_f, tm=tf)
    y_prompt = yp.reshape(bsz, seq, d)
    keep = min(WINDOW, seq)
    new_k_p = k3[:, seq - keep:].reshape(1, bsz, keep, N_KV_HEADS, HEAD_DIM).astype(cache_k_win.dtype)
    new_v_p = v3[:, seq - keep:].reshape(1, bsz, keep, N_KV_HEADS, HEAD_DIM).astype(cache_v_win.dtype)
    new_s_p = s_fin[None].astype(state_hgrn.dtype)

    cos_s, sin_s = _rope_tables(jnp.full((nb,), float(PAST_LEN), F32))
    xs = x_sample.reshape(nb, d)
    qs, ks, vs, rqs, _, rks, ris, rgs = _inproj(xs, g_pre, w_in_b, cos_s, sin_s, rec_lb, tm=nb, act_dtype=F32)
    gb = 8
    ck = cache_k_win[0].reshape(nb, w_keep, KV_WIDTH)
    cv = cache_v_win[0].reshape(nb, w_keep, KV_WIDTH)
    a_s, nk, nv = _dec_attn(qs, ks, vs, ck, cv, sinks, gb=gb)
    o_s, s_new = _dec_hgrn(rqs, rks, ris, rgs, nrm, state_hgrn[0], gb=gb)
    ys = _ffn(xs, a_s, o_s, w_out_b, w_gate_b, w_up_b, w_down_b, g_post, g_pre_f, g_post_f, tm=nb)
    y_sample = ys.reshape(nb, 1, d)
    new_k_s = nk.reshape(1, nb, w_keep, N_KV_HEADS, HEAD_DIM)
    new_v_s = nv.reshape(1, nb, w_keep, N_KV_HEADS, HEAD_DIM)
    new_s_s = s_new[None]

    return (y_prompt, y_sample, new_k_p, new_v_p, new_s_p, new_k_s, new_v_s, new_s_s)
```

```python
import functools

import jax
import jax.numpy as jnp
import numpy as np
from jax import lax
from jax.experimental import pallas as pl
from jax.experimental.pallas import tpu as pltpu

F32 = jnp.float32
BF16 = jnp.bfloat16

PAST_LEN = 16384
WINDOW = 128
HEAD_DIM = 64
N_Q_HEADS = 8
N_KV_HEADS = 2
ROT_DIM = HEAD_DIM // 4
ROPE_THETA = 500000.0
N_REC_HEADS = 4
REC_DK = 128
EPS = 1e-6

ATTN_WIDTH = N_Q_HEADS * HEAD_DIM
KV_WIDTH = N_KV_HEADS * HEAD_DIM
REC_WIDTH = N_REC_HEADS * REC_DK
LANES = 128
NEG = -1e30

VMEM_LIMIT = 56 * 1024 * 1024


def _dot(a, b):
    return jnp.dot(a, b, preferred_element_type=F32)


def _dot_nt(a, b):
    return lax.dot_general(a, b, (((1,), (1,)), ((), ())), preferred_element_type=F32)


def _dot_tn(a, b):
    return lax.dot_general(a, b, (((0,), (0,)), ((), ())), preferred_element_type=F32)


def _sigmoid(x):
    return 1.0 / (1.0 + jnp.exp(-x))


def _rms(x, g):
    return x * lax.rsqrt(jnp.mean(x * x, axis=-1, keepdims=True) + EPS) * g


def _inproj_body(x_ref, g_ref, w_ref, cos_ref, sin_ref, lbp_ref,
                 q_ref, k_ref, v_ref, rq_ref, lf_ref, rk_ref, ri_ref, rg_ref):
    h = _rms(x_ref[...], g_ref[...]).astype(BF16)
    cos = cos_ref[...]
    sin = sin_ref[...]
    lane = lax.broadcasted_iota(jnp.int32, cos.shape, 1)
    first = (lane & (HEAD_DIM - 1)) < (ROT_DIM // 2)

    def rope(z):
        partner = jnp.where(first, pltpu.roll(z, LANES - ROT_DIM // 2, 1), pltpu.roll(z, ROT_DIM // 2, 1))
        return z * cos + partner * sin

    def proj(lo, width):
        return _dot(h, w_ref[:, lo:lo + width])

    scale = HEAD_DIM ** -0.5
    for j in range(ATTN_WIDTH // LANES):
        z = rope(proj(j * LANES, LANES))
        q_ref[:, j * LANES:(j + 1) * LANES] = (z * scale).astype(q_ref.dtype)
    off = ATTN_WIDTH
    k_ref[...] = rope(proj(off, KV_WIDTH))
    off += KV_WIDTH
    v_ref[...] = proj(off, KV_WIDTH)
    off += KV_WIDTH

    z = proj(off, REC_WIDTH)
    rq_ref[...] = (z * _sigmoid(z)).astype(rq_ref.dtype)
    off += REC_WIDTH

    r = lbp_ref[...]
    e = jnp.exp(r - jnp.max(r, axis=0, keepdims=True))
    lb = e[0:1, :] / jnp.sum(e, axis=0, keepdims=True)
    z = proj(off, REC_WIDTH)
    f = lb + (1.0 - lb) * _sigmoid(z)
    lf_ref[...] = jnp.log(f)
    rk_ref[...] = (1.0 - f).astype(rk_ref.dtype)
    off += REC_WIDTH

    ri_ref[...] = proj(off, REC_WIDTH).astype(ri_ref.dtype)
    off += REC_WIDTH
    z = proj(off, REC_WIDTH)
    rg_ref[...] = (z * _sigmoid(z)).astype(rg_ref.dtype)


def _inproj(x, g, w_bf, cos, sin, rec_lb, *, tm, act_dtype):
    n, d = x.shape
    t_tab = cos.shape[0]
    n_tab = t_tab // tm
    in_w = w_bf.shape[1]
    row = lambda i: (i, 0)
    const = lambda i: (0, 0)
    outs = [
        jax.ShapeDtypeStruct((n, ATTN_WIDTH), act_dtype),
        jax.ShapeDtypeStruct((n, KV_WIDTH), F32),
        jax.ShapeDtypeStruct((n, KV_WIDTH), F32),
        jax.ShapeDtypeStruct((n, REC_WIDTH), act_dtype),
        jax.ShapeDtypeStruct((n, REC_WIDTH), F32),
        jax.ShapeDtypeStruct((n, REC_WIDTH), act_dtype),
        jax.ShapeDtypeStruct((n, REC_WIDTH), act_dtype),
        jax.ShapeDtypeStruct((n, REC_WIDTH), act_dtype),
    ]
    return pl.pallas_call(
        _inproj_body,
        grid=(n // tm,),
        in_specs=[
            pl.BlockSpec((tm, d), row),
            pl.BlockSpec((1, d), const),
            pl.BlockSpec((d, in_w), const),
            pl.BlockSpec((tm, LANES), lambda i: (i % n_tab, 0)),
            pl.BlockSpec((tm, LANES), lambda i: (i % n_tab, 0)),
            pl.BlockSpec(rec_lb.shape, const),
        ],
        out_specs=[pl.BlockSpec((tm, o.shape[1]), row) for o in outs],
        out_shape=outs,
        compiler_params=pltpu.CompilerParams(
            dimension_semantics=("arbitrary",), vmem_limit_bytes=VMEM_LIMIT),
        name="inproj",
    )(x, g, w_bf, cos, sin, rec_lb)


def _head_variants(z):
    lane = lax.broadcasted_iota(jnp.int32, z.shape, 1)
    low = lane < HEAD_DIM
    z0 = jnp.where(low, z, 0.0)
    z1 = jnp.where(low, 0.0, z)
    return {
        (0, 0): z0.astype(BF16),
        (1, 1): z1.astype(BF16),
        (0, 1): pltpu.roll(z0, HEAD_DIM, 1).astype(BF16),
        (1, 0): pltpu.roll(z1, HEAD_DIM, 1).astype(BF16),
    }


def _swa_body(sink_ref, q_ref, kc_ref, vc_ref, kp_ref, vp_ref, a_ref, *, nblk):
    i = pl.program_id(1)
    w = WINDOW
    key = lax.broadcasted_iota(jnp.int32, (w, w), 0)
    qry = lax.broadcasted_iota(jnp.int32, (w, w), 1)
    cur_ok = key <= qry
    prev_band = key > qry

    kblocks = [kp_ref[...]] + [kc_ref[j * w:(j + 1) * w, :] for j in range(nblk)]
    vblocks = [vp_ref[...]] + [vc_ref[j * w:(j + 1) * w, :] for j in range(nblk)]
    kvar = [_head_variants(z) for z in kblocks]
    vt = [z.T.astype(BF16) for z in vblocks]

    group = N_Q_HEADS // N_KV_HEADS
    for j in range(nblk):
        prev_ok = prev_band if j > 0 else jnp.logical_and(prev_band, i > 0)
        for h in range(N_KV_HEADS):
            lhs = jnp.concatenate([kvar[j][(h, 0)], kvar[j + 1][(h, 0)],
                                   kvar[j][(h, 1)], kvar[j + 1][(h, 1)]], axis=0)
            vth = jnp.concatenate([vt[j][h * HEAD_DIM:(h + 1) * HEAD_DIM, :],
                                   vt[j + 1][h * HEAD_DIM:(h + 1) * HEAD_DIM, :]], axis=1)
            for cc in range(group // 2):
                c = h * (group // 2) + cc
                qs = q_ref[j * w:(j + 1) * w, c * LANES:(c + 1) * LANES]
                st = _dot_nt(lhs, qs)
                outs = []
                for half in range(2):
                    sink = sink_ref[2 * c + half]
                    base = half * 2 * w
                    sp = jnp.where(prev_ok, st[base:base + w, :], NEG)
                    sc = jnp.where(cur_ok, st[base + w:base + 2 * w, :], NEG)
                    m = jnp.maximum(jnp.max(sp, axis=0, keepdims=True), jnp.max(sc, axis=0, keepdims=True))
                    m = jnp.maximum(m, sink)
                    pp = jnp.exp(sp - m)
                    pc = jnp.exp(sc - m)
                    den = (jnp.sum(pp, axis=0, keepdims=True) + jnp.sum(pc, axis=0, keepdims=True)
                           + jnp.exp(sink - m))
                    p = jnp.concatenate([pp, pc], axis=0).astype(BF16)
                    outs.append(_dot(vth, p) * (1.0 / den))
                ot = jnp.concatenate(outs, axis=0)
                a_ref[j * w:(j + 1) * w, c * LANES:(c + 1) * LANES] = ot.T.astype(a_ref.dtype)


def _swa_prompt(q, k, v, sinks, *, tq):
    b, t, _ = q.shape
    nblk = tq // WINDOW
    cur = lambda bi, i: (bi, i, 0)
    prev = lambda bi, i: (bi, jnp.maximum(i * nblk - 1, 0), 0)
    return pl.pallas_call(
        functools.partial(_swa_body, nblk=nblk),
        grid=(b, t // tq),
        in_specs=[
            pl.BlockSpec(memory_space=pltpu.SMEM),
            pl.BlockSpec((None, tq, ATTN_WIDTH), cur),
            pl.BlockSpec((None, tq, KV_WIDTH), cur),
            pl.BlockSpec((None, tq, KV_WIDTH), cur),
            pl.BlockSpec((None, WINDOW, KV_WIDTH), prev),
            pl.BlockSpec((None, WINDOW, KV_WIDTH), prev),
        ],
        out_specs=pl.BlockSpec((None, tq, ATTN_WIDTH), cur),
        out_shape=jax.ShapeDtypeStruct((b, t, ATTN_WIDTH), BF16),
        compiler_params=pltpu.CompilerParams(
            dimension_semantics=("arbitrary", "arbitrary"), vmem_limit_bytes=VMEM_LIMIT),
        name="swa_prompt",
    )(sinks, q, k, v, k, v)


def _split3(x):
    hi = x.astype(BF16)
    r1 = x - hi.astype(F32)
    mid = r1.astype(BF16)
    lo = (r1 - mid.astype(F32)).astype(BF16)
    return hi, mid, lo


def _level_table(c):
    t = np.arange(c)[:, None]
    s = np.arange(c)[None, :]
    x = t ^ s
    lvl = np.where(x > 0, np.floor(np.log2(np.maximum(x, 1))).astype(np.int32), 0)
    nlev = int(np.log2(c))
    return np.where(t > s, lvl, np.where(t == s, nlev, -1)).astype(np.int32)


def _hgrn_body(q_ref, lf_ref, k_ref, v_ref, g_ref, nrm_ref, lvl_ref, o_ref, sfin_ref,
               st_ref, b_ref, q32_ref, k32_ref, *, chunk, nchunk):
    i = pl.program_id(1)

    @pl.when(i == 0)
    def _():
        st_ref[...] = jnp.zeros_like(st_ref)

    c = chunk
    sub = 8
    blk = 16
    nlev = c.bit_length() - 1
    first_slab = 4
    row = lax.broadcasted_iota(jnp.int32, (c, c), 0)
    col = lax.broadcasted_iota(jnp.int32, (c, c), 1)
    tri = (col <= row).astype(BF16)
    brow_i = lax.broadcasted_iota(jnp.int32, (blk, REC_DK), 0)
    odd = (brow_i & 1) != 0
    side1 = (brow_i & 2) != 0
    side2 = (brow_i & 4) != 0
    side3 = (brow_i & 8) != 0
    low4 = (brow_i & 4) == 0
    lane = lax.broadcasted_iota(jnp.int32, (sub, c), 1)

    for ci in range(nchunk):
        sl = slice(ci * c, (ci + 1) * c)
        hi, mid, lo = _split3(lf_ref[sl, :])
        b_ref[...] = _dot(tri, hi) + _dot(tri, mid) + _dot(tri, lo)
        q32_ref[...] = q_ref[sl, :].astype(F32)
        k32_ref[...] = k_ref[sl, :].astype(F32)
        lvl = lvl_ref[...]

        for h in range(N_REC_HEADS):
            hs = slice(h * REC_DK, (h + 1) * REC_DK)

            def bcast(r, n=sub):
                return jnp.broadcast_to(b_ref[r:r + 1, hs], (n, REC_DK))

            blast = b_ref[c - 1:c, hs]
            q_in, k_out = [], []
            lev = [[] for _ in range(nlev)]
            lev_q = [[] for _ in range(nlev)]
            for r0 in range(0, c, blk):
                q = q32_ref[r0:r0 + blk, hs]
                k = k32_ref[r0:r0 + blk, hs]
                b = b_ref[r0:r0 + blk, hs]
                lf = lf_ref[ci * c + r0:ci * c + r0 + blk, hs]
                q_in.append((q * jnp.exp(b)).astype(BF16))
                k_out.append((k * jnp.exp(blast - b)).astype(BF16))
                lev[0].append((jnp.where(odd, q, k) * jnp.exp(jnp.where(odd, lf, 0.0))).astype(BF16))
                bref = jnp.concatenate([jnp.where(low4[:sub], bcast(r0 + 1), bcast(r0 + 5)),
                                        jnp.where(low4[:sub], bcast(r0 + 9), bcast(r0 + 13))], axis=0)
                lev[1].append((jnp.where(side1, q, k) * jnp.exp(-jnp.abs(b - bref))).astype(BF16))
                bref = jnp.concatenate([bcast(r0 + 3), bcast(r0 + 11)], axis=0)
                lev[2].append((jnp.where(side2, q, k) * jnp.exp(-jnp.abs(b - bref))).astype(BF16))
                bref = bcast(r0 + 7, blk)
                lev[3].append((jnp.where(side3, q, k) * jnp.exp(-jnp.abs(b - bref))).astype(BF16))
                for l in range(first_slab, nlev):
                    m = 1 << l
                    g0 = (r0 // (2 * m)) * (2 * m)
                    bref = bcast(g0 + m - 1, blk)
                    if r0 - g0 < m:
                        lev[l].append((k * jnp.exp(bref - b)).astype(BF16))
                    else:
                        piece = (q * jnp.exp(b - bref)).astype(BF16)
                        lev[l].append(piece)
                        lev_q[l].append(piece)
            cat = lambda parts: parts[0] if len(parts) == 1 else jnp.concatenate(parts, axis=0)
            q_in, k_out = cat(q_in), cat(k_out)

            sc = jnp.where(lvl == nlev, _dot_nt(q_ref[sl, hs], k_ref[sl, hs]), 0.0)
            for l in range(first_slab):
                a = cat(lev[l])
                sc = jnp.where(lvl == l, _dot_nt(a, a), sc)
            blocks = [sc[r0:r0 + sub, :] for r0 in range(0, c, sub)]
            for l in range(first_slab, nlev):
                m = 1 << l
                pq = _dot_nt(cat(lev_q[l]), cat(lev[l]))
                for gi, g0 in enumerate(range(0, c, 2 * m)):
                    in_group = jnp.logical_and(lane >= g0, lane < g0 + m)
                    for r in range(0, m, sub):
                        bi = (g0 + m + r) // sub
                        blocks[bi] = jnp.where(in_group, pq[gi * m + r:gi * m + r + sub, :], blocks[bi])
            sc = jnp.concatenate(blocks, axis=0)
            st = st_ref[h]
            oh = _dot_nt(q_in, st.astype(BF16)) + _dot(sc.astype(BF16), v_ref[sl, hs])
            st_ref[h] = st * jnp.exp(blast) + _dot_tn(v_ref[sl, hs], k_out)
            oh = oh * lax.rsqrt(jnp.mean(oh * oh, axis=-1, keepdims=True) + EPS) * nrm_ref[h:h + 1, :]
            o_ref[sl, hs] = (oh * g_ref[sl, hs].astype(F32)).astype(o_ref.dtype)

    @pl.when(i == pl.num_programs(1) - 1)
    def _():
        for h in range(N_REC_HEADS):
            sfin_ref[h] = st_ref[h].T


def _hgrn_prompt(q, lf, k, v, g, nrm, *, tc, chunk):
    b, t, w = q.shape
    cur = lambda bi, i: (bi, i, 0)
    const2 = lambda bi, i: (0, 0)
    lvl = jnp.asarray(_level_table(chunk))
    return pl.pallas_call(
        functools.partial(_hgrn_body, chunk=chunk, nchunk=tc // chunk),
        grid=(b, t // tc),
        in_specs=[
            pl.BlockSpec((None, tc, w), cur),
            pl.BlockSpec((None, tc, w), cur),
            pl.BlockSpec((None, tc, w), cur),
            pl.BlockSpec((None, tc, w), cur),
            pl.BlockSpec((None, tc, w), cur),
            pl.BlockSpec(nrm.shape, const2),
            pl.BlockSpec((chunk, chunk), const2),
        ],
        out_specs=[
            pl.BlockSpec((None, tc, w), cur),
            pl.BlockSpec((None, N_REC_HEADS, REC_DK, REC_DK), lambda bi, i: (bi, 0, 0, 0)),
        ],
        out_shape=[
            jax.ShapeDtypeStruct((b, t, w), BF16),
            jax.ShapeDtypeStruct((b, N_REC_HEADS, REC_DK, REC_DK), F32),
        ],
        scratch_shapes=[
            pltpu.VMEM((N_REC_HEADS, REC_DK, REC_DK), F32),
            pltpu.VMEM((chunk, w), F32),
            pltpu.VMEM((chunk, w), F32),
            pltpu.VMEM((chunk, w), F32),
        ],
        compiler_params=pltpu.CompilerParams(
            dimension_semantics=("arbitrary", "arbitrary"), vmem_limit_bytes=VMEM_LIMIT),
        name="hgrn_prompt",
    )(q, lf, k, v, g, nrm, lvl)


def _dec_attn_body(sink_ref, q_ref, kn_ref, vn_ref, ck_ref, cv_ref, a_ref, nk_ref, nv_ref,
                   qs_ref, os_ref, *, gb):
    group = N_Q_HEADS // N_KV_HEADS
    lane = lax.broadcasted_iota(jnp.int32, (gb, LANES), 1)
    low = lane < HEAD_DIM
    q = q_ref[...]
    for hq in range(N_Q_HEADS):
        z = q[:, (hq // 2) * LANES:(hq // 2 + 1) * LANES]
        z = jnp.where(low, z, 0.0) if hq % 2 == 0 else jnp.where(low, 0.0, z)
        if hq % 2 != hq // group:
            z = pltpu.roll(z, HEAD_DIM, 1)
        qs_ref[hq * gb:(hq + 1) * gb, :] = z

    hrow = lax.broadcasted_iota(jnp.int32, (N_Q_HEADS, 1), 0)
    sink = jnp.zeros((N_Q_HEADS, 1), F32)
    for hq in range(N_Q_HEADS):
        sink = jnp.where(hrow == hq, sink_ref[hq], sink)
    pad = jnp.zeros((8, LANES), BF16)
    w = ck_ref.shape[1]
    last = lax.broadcasted_iota(jnp.int32, (w, LANES), 0) == w - 1

    for j in range(gb):
        kk = jnp.where(last, kn_ref[j:j + 1, :], pltpu.roll(ck_ref[j], w - 1, 0))
        vv = jnp.where(last, vn_ref[j:j + 1, :], pltpu.roll(cv_ref[j], w - 1, 0))
        nk_ref[j] = kk
        nv_ref[j] = vv
        qp = qs_ref[pl.ds(j, N_Q_HEADS, stride=gb), :]
        qp = jnp.concatenate([qp.astype(BF16), pad], axis=0)
        s = _dot_nt(qp, kk.astype(BF16))[0:N_Q_HEADS, :]
        m = jnp.maximum(jnp.max(s, axis=-1, keepdims=True), sink)
        p = jnp.exp(s - m)
        den = jnp.sum(p, axis=-1, keepdims=True) + jnp.exp(sink - m)
        pb = jnp.concatenate([p.astype(BF16), pad], axis=0)
        o = _dot(pb, vv.astype(BF16))[0:N_Q_HEADS, :] / den
        os_ref[j * N_Q_HEADS:(j + 1) * N_Q_HEADS, :] = o

    for cidx in range(ATTN_WIDTH // LANES):
        parts = []
        for half in range(2):
            hq = 2 * cidx + half
            z = os_ref[pl.ds(hq, gb, stride=N_Q_HEADS), :]
            if half != hq // group:
                z = pltpu.roll(z, HEAD_DIM, 1)
            parts.append(z)
        a_ref[:, cidx * LANES:(cidx + 1) * LANES] = jnp.where(low, parts[0], parts[1]).astype(a_ref.dtype)


def _dec_attn(q, k_new, v_new, cache_k, cache_v, sinks, *, gb):
    nb = q.shape[0]
    w = cache_k.shape[1]
    row = lambda i: (i, 0)
    blk3 = lambda i: (i, 0, 0)
    return pl.pallas_call(
        functools.partial(_dec_attn_body, gb=gb),
        grid=(nb // gb,),
        in_specs=[
            pl.BlockSpec(memory_space=pltpu.SMEM),
            pl.BlockSpec((gb, ATTN_WIDTH), row),
            pl.BlockSpec((gb, KV_WIDTH), row),
            pl.BlockSpec((gb, KV_WIDTH), row),
            pl.BlockSpec((gb, w, KV_WIDTH), blk3),
            pl.BlockSpec((gb, w, KV_WIDTH), blk3),
        ],
        out_specs=[
            pl.BlockSpec((gb, ATTN_WIDTH), row),
            pl.BlockSpec((gb, w, KV_WIDTH), blk3),
            pl.BlockSpec((gb, w, KV_WIDTH), blk3),
        ],
        out_shape=[
            jax.ShapeDtypeStruct((nb, ATTN_WIDTH), BF16),
            jax.ShapeDtypeStruct(cache_k.shape, cache_k.dtype),
            jax.ShapeDtypeStruct(cache_v.shape, cache_v.dtype),
        ],
        scratch_shapes=[
            pltpu.VMEM((N_Q_HEADS * gb, LANES), F32),
            pltpu.VMEM((N_Q_HEADS * gb, LANES), F32),
        ],
        compiler_params=pltpu.CompilerParams(
            dimension_semantics=("arbitrary",), vmem_limit_bytes=VMEM_LIMIT),
        name="dec_attn",
    )(sinks, q, k_new, v_new, cache_k, cache_v)


def _dec_hgrn_body(q_ref, k_ref, i_ref, g_ref, nrm_ref, s_ref, o_ref, sn_ref, os_ref, *, gb):
    dk = REC_DK
    zpad = jnp.zeros((dk - gb, dk), F32)
    for h in range(N_REC_HEADS):
        hs = slice(h * dk, (h + 1) * dk)
        qt = jnp.concatenate([q_ref[:, hs], zpad], axis=0).T
        kt = jnp.concatenate([k_ref[:, hs], zpad], axis=0).T
        for j in range(gb):
            kcol = kt[:, j:j + 1]
            s_new = s_ref[j, h] * (1.0 - kcol) + kcol * i_ref[j:j + 1, hs]
            sn_ref[j, h] = s_new
            os_ref[j:j + 1, hs] = jnp.sum(s_new * qt[:, j:j + 1], axis=0, keepdims=True)
    for h in range(N_REC_HEADS):
        hs = slice(h * dk, (h + 1) * dk)
        oh = os_ref[:, hs]
        oh = oh * lax.rsqrt(jnp.mean(oh * oh, axis=-1, keepdims=True) + EPS) * nrm_ref[h:h + 1, :]
        o_ref[:, hs] = (oh * g_ref[:, hs]).astype(o_ref.dtype)


def _dec_hgrn(q, k, iv, g, nrm, state, *, gb):
    nb, w = q.shape
    row = lambda i: (i, 0)
    blk4 = lambda i: (i, 0, 0, 0)
    return pl.pallas_call(
        functools.partial(_dec_hgrn_body, gb=gb),
        grid=(nb // gb,),
        in_specs=[
            pl.BlockSpec((gb, w), row),
            pl.BlockSpec((gb, w), row),
            pl.BlockSpec((gb, w), row),
            pl.BlockSpec((gb, w), row),
            pl.BlockSpec(nrm.shape, lambda i: (0, 0)),
            pl.BlockSpec((gb,) + state.shape[1:], blk4),
        ],
        out_specs=[
            pl.BlockSpec((gb, w), row),
            pl.BlockSpec((gb,) + state.shape[1:], blk4),
        ],
        out_shape=[
            jax.ShapeDtypeStruct((nb, w), BF16),
            jax.ShapeDtypeStruct(state.shape, state.dtype),
        ],
        scratch_shapes=[pltpu.VMEM((gb, w), F32)],
        compiler_params=pltpu.CompilerParams(
            dimension_semantics=("arbitrary",), vmem_limit_bytes=VMEM_LIMIT),
        name="dec_hgrn",
    )(q, k, iv, g, nrm, state)


def _ffn_body(x_ref, a_ref, o_ref, wo_ref, wg_ref, wu_ref, wd_ref, g1_ref, g2_ref, g3_ref, y_ref):
    aw = a_ref.shape[1]
    mix = _dot(a_ref[...], wo_ref[0:aw, :]) + _dot(o_ref[...], wo_ref[aw:, :])
    x1 = x_ref[...] + _rms(mix, g1_ref[...])
    h2 = _rms(x1, g2_ref[...]).astype(BF16)
    gate = _dot(h2, wg_ref[...])
    up = _dot(h2, wu_ref[...])
    act = (gate * _sigmoid(gate) * up).astype(BF16)
    ffn = _dot(act, wd_ref[...])
    y_ref[...] = x1 + _rms(ffn, g3_ref[...])


def _ffn(x, a, o, wo, wg, wu, wd, g1, g2, g3, *, tm):
    n, d = x.shape
    row = lambda i: (i, 0)
    const = lambda i: (0, 0)
    resident = lambda arr: pl.BlockSpec(arr.shape, const, pipeline_mode=pl.Buffered(1))
    return pl.pallas_call(
        _ffn_body,
        grid=(n // tm,),
        in_specs=[
            pl.BlockSpec((tm, d), row),
            pl.BlockSpec((tm, a.shape[1]), row),
            pl.BlockSpec((tm, o.shape[1]), row),
            resident(wo), resident(wg), resident(wu), resident(wd),
            pl.BlockSpec((1, d), const), pl.BlockSpec((1, d), const), pl.BlockSpec((1, d), const),
        ],
        out_specs=pl.BlockSpec((tm, d), row),
        out_shape=jax.ShapeDtypeStruct((n, d), x.dtype),
        compiler_params=pltpu.CompilerParams(
            dimension_semantics=("arbitrary",), vmem_limit_bytes=VMEM_LIMIT),
        name="ffn",
    )(x, a, o, wo, wg, wu, wd, g1, g2, g3)


def _rope_tables(pos):
    half = ROT_DIM // 2
    inv_freq = jnp.exp(-jnp.log(jnp.asarray(ROPE_THETA, F32)) * jnp.arange(half, dtype=F32) * (2.0 / ROT_DIM))
    d = np.arange(LANES) % HEAD_DIM
    rot = d < ROT_DIM
    freq = jnp.where(rot, inv_freq[d % half], 0.0)
    sign = jnp.asarray(np.where(d < half, -1.0, np.where(rot, 1.0, 0.0)), F32)
    ang = pos[:, None] * freq[None, :]
    return jnp.cos(ang), jnp.sin(ang) * sign[None, :]


def _pick_tile(n, want):
    t = min(n, want)
    while n % t:
        t //= 2
    return t


def kernel(x_prompt, x_sample, cache_k_win, cache_v_win, state_hgrn, w_in, w_out, w_gate, w_up, w_down,
           norm_pre_mix, norm_post_mix, norm_pre_ffn, norm_post_ffn, attn_sinks, rec_lb, rec_out_norm):
    depth = w_in.shape[0]
    assert depth == 1, "single-layer step"
    bsz, seq, d = x_prompt.shape
    nb, dec_seq, _ = x_sample.shape
    assert dec_seq == 1 and seq % WINDOW == 0
    w_keep = cache_k_win.shape[2]
    assert w_keep == WINDOW

    w_in_b = w_in[0].astype(BF16)
    w_out_b = w_out[0].astype(BF16)
    w_gate_b = w_gate[0].astype(BF16)
    w_up_b = w_up[0].astype(BF16)
    w_down_b = w_down[0].astype(BF16)
    g_pre, g_post, g_pre_f, g_post_f = norm_pre_mix, norm_post_mix, norm_pre_ffn, norm_post_ffn
    sinks = attn_sinks[0].astype(F32)
    nrm = rec_out_norm[0].astype(F32)

    tm = _pick_tile(seq, 512)
    cos_p, sin_p = _rope_tables(jnp.arange(seq, dtype=F32))
    xp = x_prompt.reshape(bsz * seq, d)
    q, k, v, rq, lf, rk, ri, rg = _inproj(xp, g_pre, w_in_b, cos_p, sin_p, rec_lb, tm=tm, act_dtype=BF16)
    r3 = lambda z: z.reshape(bsz, seq, z.shape[-1])
    k3, v3 = r3(k), r3(v)
    a = _swa_prompt(r3(q), k3, v3, sinks, tq=tm)
    chunk = 128
    o, s_fin = _hgrn_prompt(r3(rq), r3(lf), r3(rk), r3(ri), r3(rg), nrm, tc=tm, chunk=chunk)
    tf = _pick_tile(bsz * seq, 256)
    yp = _ffn(xp, a.reshape(bsz * seq, -1), o.reshape(bsz * seq, -1), w_out_b, w_gate_b, w_up_b, w_down_b,
              g_post, g_pre_f, g_post_f, tm=tf)
    y_prompt = yp.reshape(bsz, seq, d)
    keep = min(WINDOW, seq)
    new_k_p = k3[:, seq - keep:].reshape(1, bsz, keep, N_KV_HEADS, HEAD_DIM).astype(cache_k_win.dtype)
    new_v_p = v3[:, seq - keep:].reshape(1, bsz, keep, N_KV_HEADS, HEAD_DIM).astype(cache_v_win.dtype)
    new_s_p = s_fin[None].astype(state_hgrn.dtype)

    cos_s, sin_s = _rope_tables(jnp.full((nb,), float(PAST_LEN), F32))
    xs = x_sample.reshape(nb, d)
    qs, ks, vs, rqs, _, rks, ris, rgs = _inproj(xs, g_pre, w_in_b, cos_s, sin_s, rec_lb, tm=nb, act_dtype=F32)
    gb = 8
    ck = cache_k_win[0].reshape(nb, w_keep, KV_WIDTH)
    cv = cache_v_win[0].reshape(nb, w_keep, KV_WIDTH)
    a_s, nk, nv = _dec_attn(qs, ks, vs, ck, cv, sinks, gb=gb)
    o_s, s_new = _dec_hgrn(rqs, rks, ris, rgs, nrm, state_hgrn[0], gb=gb)
    ys = _ffn(xs, a_s, o_s, w_out_b, w_gate_b, w_up_b, w_down_b, g_post, g_pre_f, g_post_f, tm=nb)
    y_sample = ys.reshape(nb, 1, d)
    new_k_s = nk.reshape(1, nb, w_keep, N_KV_HEADS, HEAD_DIM)
    new_v_s = nv.reshape(1, nb, w_keep, N_KV_HEADS, HEAD_DIM)
    new_s_s = s_new[None]

    return (y_prompt, y_sample, new_k_p, new_v_p, new_s_p, new_k_s, new_v_s, new_s_s)
```

```python
import functools

import jax
import jax.numpy as jnp
import numpy as np
from jax import lax
from jax.experimental import pallas as pl
from jax.experimental.pallas import tpu as pltpu

F32 = jnp.float32
BF16 = jnp.bfloat16

PAST_LEN = 16384
WINDOW = 128
HEAD_DIM = 64
N_Q_HEADS = 8
N_KV_HEADS = 2
ROT_DIM = HEAD_DIM // 4
ROPE_THETA = 500000.0
N_REC_HEADS = 4
REC_DK = 128
EPS = 1e-6

ATTN_WIDTH = N_Q_HEADS * HEAD_DIM
KV_WIDTH = N_KV_HEADS * HEAD_DIM
REC_WIDTH = N_REC_HEADS * REC_DK
LANES = 128
NEG = -1e30

VMEM_LIMIT = 56 * 1024 * 1024


def _dot(a, b):
    return jnp.dot(a, b, preferred_element_type=F32)


def _dot_nt(a, b):
    return lax.dot_general(a, b, (((1,), (1,)), ((), ())), preferred_element_type=F32)


def _dot_tn(a, b):
    return lax.dot_general(a, b, (((0,), (0,)), ((), ())), preferred_element_type=F32)


def _sigmoid(x):
    return 1.0 / (1.0 + jnp.exp(-x))


def _rms(x, g):
    return x * lax.rsqrt(jnp.mean(x * x, axis=-1, keepdims=True) + EPS) * g


def _inproj_body(x_ref, g_ref, w_ref, cos_ref, sin_ref, lbp_ref,
                 q_ref, k_ref, v_ref, rq_ref, lf_ref, rk_ref, ri_ref, rg_ref):
    h = _rms(x_ref[...], g_ref[...]).astype(BF16)
    cos = cos_ref[...]
    sin = sin_ref[...]
    lane = lax.broadcasted_iota(jnp.int32, cos.shape, 1)
    first = (lane & (HEAD_DIM - 1)) < (ROT_DIM // 2)

    def rope(z):
        partner = jnp.where(first, pltpu.roll(z, LANES - ROT_DIM // 2, 1), pltpu.roll(z, ROT_DIM // 2, 1))
        return z * cos + partner * sin

    def proj(lo, width):
        return _dot(h, w_ref[:, lo:lo + width])

    scale = HEAD_DIM ** -0.5
    zq = proj(0, ATTN_WIDTH)
    for j in range(ATTN_WIDTH // LANES):
        z = rope(zq[:, j * LANES:(j + 1) * LANES])
        q_ref[:, j * LANES:(j + 1) * LANES] = (z * scale).astype(q_ref.dtype)
    off = ATTN_WIDTH
    zkv = proj(off, 2 * KV_WIDTH)
    k_ref[...] = rope(zkv[:, :KV_WIDTH])
    v_ref[...] = zkv[:, KV_WIDTH:]
    off += 2 * KV_WIDTH

    z = proj(off, REC_WIDTH)
    rq_ref[...] = (z * _sigmoid(z)).astype(rq_ref.dtype)
    off += REC_WIDTH

    r = lbp_ref[...]
    e = jnp.exp(r - jnp.max(r, axis=0, keepdims=True))
    lb = e[0:1, :] / jnp.sum(e, axis=0, keepdims=True)
    z = proj(off, REC_WIDTH)
    f = lb + (1.0 - lb) * _sigmoid(z)
    lf_ref[...] = jnp.log(f)
    rk_ref[...] = (1.0 - f).astype(rk_ref.dtype)
    off += REC_WIDTH

    ri_ref[...] = proj(off, REC_WIDTH).astype(ri_ref.dtype)
    off += REC_WIDTH
    z = proj(off, REC_WIDTH)
    rg_ref[...] = (z * _sigmoid(z)).astype(rg_ref.dtype)


def _inproj(x, g, w_bf, cos, sin, rec_lb, *, tm, act_dtype):
    n, d = x.shape
    t_tab = cos.shape[0]
    n_tab = t_tab // tm
    in_w = w_bf.shape[1]
    row = lambda i: (i, 0)
    const = lambda i: (0, 0)
    outs = [
        jax.ShapeDtypeStruct((n, ATTN_WIDTH), act_dtype),
        jax.ShapeDtypeStruct((n, KV_WIDTH), F32),
        jax.ShapeDtypeStruct((n, KV_WIDTH), F32),
        jax.ShapeDtypeStruct((n, REC_WIDTH), act_dtype),
        jax.ShapeDtypeStruct((n, REC_WIDTH), F32),
        jax.ShapeDtypeStruct((n, REC_WIDTH), act_dtype),
        jax.ShapeDtypeStruct((n, REC_WIDTH), act_dtype),
        jax.ShapeDtypeStruct((n, REC_WIDTH), act_dtype),
    ]
    return pl.pallas_call(
        _inproj_body,
        grid=(n // tm,),
        in_specs=[
            pl.BlockSpec((tm, d), row),
            pl.BlockSpec((1, d), const),
            pl.BlockSpec((d, in_w), const),
            pl.BlockSpec((tm, LANES), lambda i: (i % n_tab, 0)),
            pl.BlockSpec((tm, LANES), lambda i: (i % n_tab, 0)),
            pl.BlockSpec(rec_lb.shape, const),
        ],
        out_specs=[pl.BlockSpec((tm, o.shape[1]), row) for o in outs],
        out_shape=outs,
        compiler_params=pltpu.CompilerParams(
            dimension_semantics=("arbitrary",), vmem_limit_bytes=VMEM_LIMIT),
        name="inproj",
    )(x, g, w_bf, cos, sin, rec_lb)


def _head_variants(z):
    lane = lax.broadcasted_iota(jnp.int32, z.shape, 1)
    low = lane < HEAD_DIM
    z0 = jnp.where(low, z, 0.0)
    z1 = jnp.where(low, 0.0, z)
    return {
        (0, 0): z0.astype(BF16),
        (1, 1): z1.astype(BF16),
        (0, 1): pltpu.roll(z0, HEAD_DIM, 1).astype(BF16),
        (1, 0): pltpu.roll(z1, HEAD_DIM, 1).astype(BF16),
    }


def _swa_steps(nblk):
    return (nblk + 1) + nblk * (N_Q_HEADS // 2) * 4


def _swa_tile(sink_ref, q_ref, kc_ref, vc_ref, kp_ref, vp_ref, a_ref, has_prev, nblk):
    w = WINDOW
    key = lax.broadcasted_iota(jnp.int32, (w, w), 0)
    qry = lax.broadcasted_iota(jnp.int32, (w, w), 1)
    cur_ok = key <= qry
    prev_band = key > qry

    kvar, vt = [], []
    for j in range(-1, nblk):
        kblk = kp_ref[...] if j < 0 else kc_ref[j * w:(j + 1) * w, :]
        vblk = vp_ref[...] if j < 0 else vc_ref[j * w:(j + 1) * w, :]
        kvar.append(_head_variants(kblk))
        vt.append(vblk.T.astype(BF16))
        yield

    group = N_Q_HEADS // N_KV_HEADS
    for j in range(nblk):
        prev_ok = prev_band if j > 0 else jnp.logical_and(prev_band, has_prev)
        for h in range(N_KV_HEADS):
            lhs = jnp.concatenate([kvar[j][(h, 0)], kvar[j + 1][(h, 0)],
                                   kvar[j][(h, 1)], kvar[j + 1][(h, 1)]], axis=0)
            vth = jnp.concatenate([vt[j][h * HEAD_DIM:(h + 1) * HEAD_DIM, :],
                                   vt[j + 1][h * HEAD_DIM:(h + 1) * HEAD_DIM, :]], axis=1)
            for cc in range(group // 2):
                c = h * (group // 2) + cc
                qs = q_ref[j * w:(j + 1) * w, c * LANES:(c + 1) * LANES]
                st = _dot_nt(lhs, qs)
                yield
                outs = []
                for half in range(2):
                    sink = sink_ref[2 * c + half]
                    base = half * 2 * w
                    sp = jnp.where(prev_ok, st[base:base + w, :], NEG)
                    sc = jnp.where(cur_ok, st[base + w:base + 2 * w, :], NEG)
                    m = jnp.maximum(jnp.max(sp, axis=0, keepdims=True), jnp.max(sc, axis=0, keepdims=True))
                    m = jnp.maximum(m, sink)
                    pp = jnp.exp(sp - m)
                    pc = jnp.exp(sc - m)
                    den = (jnp.sum(pp, axis=0, keepdims=True) + jnp.sum(pc, axis=0, keepdims=True)
                           + jnp.exp(sink - m))
                    p = jnp.concatenate([pp, pc], axis=0).astype(BF16)
                    outs.append(_dot(vth, p) * (1.0 / den))
                    yield
                ot = jnp.concatenate(outs, axis=0)
                a_ref[j * w:(j + 1) * w, c * LANES:(c + 1) * LANES] = ot.T.astype(a_ref.dtype)
                yield


def _split3(x):
    hi = x.astype(BF16)
    r1 = x - hi.astype(F32)
    mid = r1.astype(BF16)
    lo = (r1 - mid.astype(F32)).astype(BF16)
    return hi, mid, lo


def _level_table(c):
    t = np.arange(c)[:, None]
    s = np.arange(c)[None, :]
    x = t ^ s
    lvl = np.where(x > 0, np.floor(np.log2(np.maximum(x, 1))).astype(np.int32), 0)
    nlev = int(np.log2(c))
    return np.where(t > s, lvl, np.where(t == s, nlev, -1)).astype(np.int32)


def _hgrn_steps(chunk, nchunk):
    return nchunk * (1 + N_REC_HEADS * (chunk // 32 + 3))


def _hgrn_tile(q_ref, lf_ref, k_ref, v_ref, g_ref, nrm_ref, lvl_ref, o_ref,
               st_ref, b_ref, q32_ref, k32_ref, chunk, nchunk):
    c = chunk
    sub = 8
    blk = 16
    nlev = c.bit_length() - 1
    first_slab = 4
    row = lax.broadcasted_iota(jnp.int32, (c, c), 0)
    col = lax.broadcasted_iota(jnp.int32, (c, c), 1)
    tri = (col <= row).astype(BF16)
    brow_i = lax.broadcasted_iota(jnp.int32, (blk, REC_DK), 0)
    odd = (brow_i & 1) != 0
    side1 = (brow_i & 2) != 0
    side2 = (brow_i & 4) != 0
    side3 = (brow_i & 8) != 0
    low4 = (brow_i & 4) == 0
    lane = lax.broadcasted_iota(jnp.int32, (sub, c), 1)

    for ci in range(nchunk):
        sl = slice(ci * c, (ci + 1) * c)
        hi, mid, lo = _split3(lf_ref[sl, :])
        b_ref[...] = _dot(tri, hi) + _dot(tri, mid) + _dot(tri, lo)
        q32_ref[...] = q_ref[sl, :].astype(F32)
        k32_ref[...] = k_ref[sl, :].astype(F32)
        lvl = lvl_ref[...]
        yield

        for h in range(N_REC_HEADS):
            hs = slice(h * REC_DK, (h + 1) * REC_DK)

            def bcast(r, n=sub):
                return jnp.broadcast_to(b_ref[r:r + 1, hs], (n, REC_DK))

            blast = b_ref[c - 1:c, hs]
            q_in, k_out = [], []
            lev = [[] for _ in range(nlev)]
            lev_q = [[] for _ in range(nlev)]
            for r0 in range(0, c, blk):
                q = q32_ref[r0:r0 + blk, hs]
                k = k32_ref[r0:r0 + blk, hs]
                b = b_ref[r0:r0 + blk, hs]
                lf = lf_ref[ci * c + r0:ci * c + r0 + blk, hs]
                q_in.append((q * jnp.exp(b)).astype(BF16))
                k_out.append((k * jnp.exp(blast - b)).astype(BF16))
                lev[0].append((jnp.where(odd, q, k) * jnp.exp(jnp.where(odd, lf, 0.0))).astype(BF16))
                bref = jnp.concatenate([jnp.where(low4[:sub], bcast(r0 + 1), bcast(r0 + 5)),
                                        jnp.where(low4[:sub], bcast(r0 + 9), bcast(r0 + 13))], axis=0)
                lev[1].append((jnp.where(side1, q, k) * jnp.exp(-jnp.abs(b - bref))).astype(BF16))
                bref = jnp.concatenate([bcast(r0 + 3), bcast(r0 + 11)], axis=0)
                lev[2].append((jnp.where(side2, q, k) * jnp.exp(-jnp.abs(b - bref))).astype(BF16))
                bref = bcast(r0 + 7, blk)
                lev[3].append((jnp.where(side3, q, k) * jnp.exp(-jnp.abs(b - bref))).astype(BF16))
                for l in range(first_slab, nlev):
                    m = 1 << l
                    g0 = (r0 // (2 * m)) * (2 * m)
                    bref = bcast(g0 + m - 1, blk)
                    if r0 - g0 < m:
                        lev[l].append((k * jnp.exp(bref - b)).astype(BF16))
                    else:
                        piece = (q * jnp.exp(b - bref)).astype(BF16)
                        lev[l].append(piece)
                        lev_q[l].append(piece)
                if r0 % (2 * blk) == blk:
                    yield
            cat = lambda parts: parts[0] if len(parts) == 1 else jnp.concatenate(parts, axis=0)
            q_in, k_out = cat(q_in), cat(k_out)

            sc = jnp.where(lvl == nlev, _dot_nt(q_ref[sl, hs], k_ref[sl, hs]), 0.0)
            for l in range(first_slab):
                a = cat(lev[l])
                sc = jnp.where(lvl == l, _dot_nt(a, a), sc)
            yield
            blocks = [sc[r0:r0 + sub, :] for r0 in range(0, c, sub)]
            for l in range(first_slab, nlev):
                m = 1 << l
                pq = _dot_nt(cat(lev_q[l]), cat(lev[l]))
                for gi, g0 in enumerate(range(0, c, 2 * m)):
                    in_group = jnp.logical_and(lane >= g0, lane < g0 + m)
                    for r in range(0, m, sub):
                        bi = (g0 + m + r) // sub
                        blocks[bi] = jnp.where(in_group, pq[gi * m + r:gi * m + r + sub, :], blocks[bi])
            sc = jnp.concatenate(blocks, axis=0)
            yield
            st = st_ref[h]
            oh = _dot_nt(q_in, st.astype(BF16)) + _dot(sc.astype(BF16), v_ref[sl, hs])
            st_ref[h] = st * jnp.exp(blast) + _dot_tn(v_ref[sl, hs], k_out)
            oh = oh * lax.rsqrt(jnp.mean(oh * oh, axis=-1, keepdims=True) + EPS) * nrm_ref[h:h + 1, :]
            o_ref[sl, hs] = (oh * g_ref[sl, hs].astype(F32)).astype(o_ref.dtype)
            yield


def _ffn_rows(x_ref, a_ref, o_ref, wo_ref, wg_ref, wu_ref, wd_ref, g1_ref, g2_ref, g3_ref, y_ref, rows):
    aw = a_ref.shape[1]
    mix = _dot(a_ref[rows, :], wo_ref[0:aw, :]) + _dot(o_ref[rows, :], wo_ref[aw:, :])
    x1 = x_ref[rows, :] + _rms(mix, g1_ref[...])
    h2 = _rms(x1, g2_ref[...]).astype(BF16)
    gate = _dot(h2, wg_ref[...])
    up = _dot(h2, wu_ref[...])
    act = (gate * _sigmoid(gate) * up).astype(BF16)
    ffn = _dot(act, wd_ref[...])
    y_ref[rows, :] = x1 + _rms(ffn, g3_ref[...])


FFN_COLS = 256
NORM_ROWS = 128


def _ffn_steps(tm, d_ff):
    return 2 + 2 * (tm // NORM_ROWS) + 3 * (d_ff // FFN_COLS)


def _ffn_tile(x_ref, a_ref, o_ref, wo_ref, wg_ref, wu_ref, wd_ref, g1_ref, g2_ref, g3_ref, y_ref,
              x1_ref, h2_ref, acc_ref):
    tm = x_ref.shape[0]
    aw = a_ref.shape[1]
    d_ff = wg_ref.shape[1]
    mix = _dot(a_ref[...], wo_ref[0:aw, :])
    yield
    mix = mix + _dot(o_ref[...], wo_ref[aw:, :])
    yield
    for r0 in range(0, tm, NORM_ROWS):
        rows = slice(r0, r0 + NORM_ROWS)
        x1 = x_ref[rows, :] + _rms(mix[rows, :], g1_ref[...])
        x1_ref[rows, :] = x1
        h2_ref[rows, :] = _rms(x1, g2_ref[...]).astype(BF16)
        yield
    for c0 in range(0, d_ff, FFN_COLS):
        cols = slice(c0, c0 + FFN_COLS)
        gate = _dot(h2_ref[...], wg_ref[:, cols])
        yield
        up = _dot(h2_ref[...], wu_ref[:, cols])
        yield
        part = _dot((gate * _sigmoid(gate) * up).astype(BF16), wd_ref[cols, :])
        acc_ref[...] = part if c0 == 0 else acc_ref[...] + part
        yield
    for r0 in range(0, tm, NORM_ROWS):
        rows = slice(r0, r0 + NORM_ROWS)
        y_ref[rows, :] = x1_ref[rows, :] + _rms(acc_ref[rows, :], g3_ref[...])
        yield


def _interleave(*streams):
    live = [[0, n, gen] for gen, n in streams]
    while live:
        cur = min(live, key=lambda e: e[0] / e[1])
        try:
            next(cur[2])
            cur[0] += 1
        except StopIteration:
            live.remove(cur)


def _mixffn_body(sink_ref, q_ref, kc_ref, vc_ref, kp_ref, vp_ref, rq_ref, lf_ref, rk_ref, ri_ref, rg_ref,
                 nrm_ref, lvl_ref, x_ref, wo_ref, wg_ref, wu_ref, wd_ref, g1_ref, g2_ref, g3_ref,
                 y_ref, sfin_ref, a_scr, o_scr, x1_scr, h2_scr, acc_scr, st_ref, b_ref, q32_ref, k32_ref,
                 *, tiles_per_seq, ntiles, nblk, chunk, nchunk):
    g = pl.program_id(0)
    real = g < ntiles
    t_in = jnp.minimum(g, ntiles - 1) % tiles_per_seq

    @pl.when(g == 0)
    def _():
        a_scr[...] = jnp.zeros_like(a_scr)
        o_scr[...] = jnp.zeros_like(o_scr)

    @pl.when(jnp.logical_and(real, t_in == 0))
    def _():
        st_ref[...] = jnp.zeros_like(st_ref)

    tm = x_ref.shape[0]
    _interleave(
        (_ffn_tile(x_ref, a_scr, o_scr, wo_ref, wg_ref, wu_ref, wd_ref, g1_ref, g2_ref, g3_ref, y_ref,
                   x1_scr, h2_scr, acc_scr), _ffn_steps(tm, wg_ref.shape[1])),
        (_swa_tile(sink_ref, q_ref, kc_ref, vc_ref, kp_ref, vp_ref, a_scr, t_in > 0, nblk), _swa_steps(nblk)),
        (_hgrn_tile(rq_ref, lf_ref, rk_ref, ri_ref, rg_ref, nrm_ref, lvl_ref, o_scr,
                    st_ref, b_ref, q32_ref, k32_ref, chunk, nchunk), _hgrn_steps(chunk, nchunk)),
    )

    @pl.when(jnp.logical_and(real, t_in == tiles_per_seq - 1))
    def _():
        for h in range(N_REC_HEADS):
            sfin_ref[h] = st_ref[h].T


def _mixffn(x, q, k, v, rq, lf, rk, ri, rg, sinks, nrm, wo, wg, wu, wd, g1, g2, g3, *, bsz, tm, chunk):
    n, d = x.shape
    ntiles = n // tm
    tiles_per_seq = ntiles // bsz
    nblk = tm // WINDOW
    w = rq.shape[1]
    lvl = jnp.asarray(_level_table(chunk))
    mix = lambda g: (jnp.minimum(g, ntiles - 1), 0)
    prev = lambda g: (jnp.maximum(jnp.minimum(g, ntiles - 1) * nblk - 1, 0), 0)
    ffn = lambda g: (jnp.maximum(g - 1, 0), 0)
    const = lambda g: (0, 0)
    resident = lambda arr: pl.BlockSpec(arr.shape, const, pipeline_mode=pl.Buffered(1))
    return pl.pallas_call(
        functools.partial(_mixffn_body, tiles_per_seq=tiles_per_seq, ntiles=ntiles, nblk=nblk,
                          chunk=chunk, nchunk=tm // chunk),
        grid=(ntiles + 1,),
        in_specs=[
            pl.BlockSpec(memory_space=pltpu.SMEM),
            pl.BlockSpec((tm, ATTN_WIDTH), mix),
            pl.BlockSpec((tm, KV_WIDTH), mix),
            pl.BlockSpec((tm, KV_WIDTH), mix),
            pl.BlockSpec((WINDOW, KV_WIDTH), prev),
            pl.BlockSpec((WINDOW, KV_WIDTH), prev),
            pl.BlockSpec((tm, w), mix),
            pl.BlockSpec((tm, w), mix),
            pl.BlockSpec((tm, w), mix),
            pl.BlockSpec((tm, w), mix),
            pl.BlockSpec((tm, w), mix),
            pl.BlockSpec(nrm.shape, const),
            pl.BlockSpec((chunk, chunk), const),
            pl.BlockSpec((tm, d), ffn),
            resident(wo), resident(wg), resident(wu), resident(wd),
            pl.BlockSpec((1, d), const), pl.BlockSpec((1, d), const), pl.BlockSpec((1, d), const),
        ],
        out_specs=[
            pl.BlockSpec((tm, d), ffn),
            pl.BlockSpec((None, N_REC_HEADS, REC_DK, REC_DK),
                         lambda g: (jnp.minimum(g, ntiles - 1) // tiles_per_seq, 0, 0, 0)),
        ],
        out_shape=[
            jax.ShapeDtypeStruct((n, d), x.dtype),
            jax.ShapeDtypeStruct((bsz, N_REC_HEADS, REC_DK, REC_DK), F32),
        ],
        scratch_shapes=[
            pltpu.VMEM((tm, ATTN_WIDTH), BF16),
            pltpu.VMEM((tm, w), BF16),
            pltpu.VMEM((tm, d), F32),
            pltpu.VMEM((tm, d), BF16),
            pltpu.VMEM((tm, d), F32),
            pltpu.VMEM((N_REC_HEADS, REC_DK, REC_DK), F32),
            pltpu.VMEM((chunk, w), F32),
            pltpu.VMEM((chunk, w), F32),
            pltpu.VMEM((chunk, w), F32),
        ],
        compiler_params=pltpu.CompilerParams(
            dimension_semantics=("arbitrary",), vmem_limit_bytes=VMEM_LIMIT),
        name="mixffn",
    )(sinks, q, k, v, k, v, rq, lf, rk, ri, rg, nrm, lvl, x, wo, wg, wu, wd, g1, g2, g3)


def _dec_attn_body(sink_ref, q_ref, kn_ref, vn_ref, ck_ref, cv_ref, a_ref, nk_ref, nv_ref,
                   qs_ref, os_ref, *, gb):
    group = N_Q_HEADS // N_KV_HEADS
    lane = lax.broadcasted_iota(jnp.int32, (gb, LANES), 1)
    low = lane < HEAD_DIM
    q = q_ref[...]
    for hq in range(N_Q_HEADS):
        z = q[:, (hq // 2) * LANES:(hq // 2 + 1) * LANES]
        z = jnp.where(low, z, 0.0) if hq % 2 == 0 else jnp.where(low, 0.0, z)
        if hq % 2 != hq // group:
            z = pltpu.roll(z, HEAD_DIM, 1)
        qs_ref[hq * gb:(hq + 1) * gb, :] = z

    hrow = lax.broadcasted_iota(jnp.int32, (N_Q_HEADS, 1), 0)
    sink = jnp.zeros((N_Q_HEADS, 1), F32)
    for hq in range(N_Q_HEADS):
        sink = jnp.where(hrow == hq, sink_ref[hq], sink)
    pad = jnp.zeros((8, LANES), BF16)
    w = ck_ref.shape[1]
    last = lax.broadcasted_iota(jnp.int32, (w, LANES), 0) == w - 1

    for j in range(gb):
        kk = jnp.where(last, kn_ref[j:j + 1, :], pltpu.roll(ck_ref[j], w - 1, 0))
        vv = jnp.where(last, vn_ref[j:j + 1, :], pltpu.roll(cv_ref[j], w - 1, 0))
        nk_ref[j] = kk
        nv_ref[j] = vv
        qp = qs_ref[pl.ds(j, N_Q_HEADS, stride=gb), :]
        qp = jnp.concatenate([qp.astype(BF16), pad], axis=0)
        s = _dot_nt(qp, kk.astype(BF16))[0:N_Q_HEADS, :]
        m = jnp.maximum(jnp.max(s, axis=-1, keepdims=True), sink)
        p = jnp.exp(s - m)
        den = jnp.sum(p, axis=-1, keepdims=True) + jnp.exp(sink - m)
        pb = jnp.concatenate([p.astype(BF16), pad], axis=0)
        o = _dot(pb, vv.astype(BF16))[0:N_Q_HEADS, :] / den
        os_ref[j * N_Q_HEADS:(j + 1) * N_Q_HEADS, :] = o

    for cidx in range(ATTN_WIDTH // LANES):
        parts = []
        for half in range(2):
            hq = 2 * cidx + half
            z = os_ref[pl.ds(hq, gb, stride=N_Q_HEADS), :]
            if half != hq // group:
                z = pltpu.roll(z, HEAD_DIM, 1)
            parts.append(z)
        a_ref[:, cidx * LANES:(cidx + 1) * LANES] = jnp.where(low, parts[0], parts[1]).astype(a_ref.dtype)


def _dec_attn(q, k_new, v_new, cache_k, cache_v, sinks, *, gb):
    nb = q.shape[0]
    w = cache_k.shape[1]
    row = lambda i: (i, 0)
    blk3 = lambda i: (i, 0, 0)
    return pl.pallas_call(
        functools.partial(_dec_attn_body, gb=gb),
        grid=(nb // gb,),
        in_specs=[
            pl.BlockSpec(memory_space=pltpu.SMEM),
            pl.BlockSpec((gb, ATTN_WIDTH), row),
            pl.BlockSpec((gb, KV_WIDTH), row),
            pl.BlockSpec((gb, KV_WIDTH), row),
            pl.BlockSpec((gb, w, KV_WIDTH), blk3),
            pl.BlockSpec((gb, w, KV_WIDTH), blk3),
        ],
        out_specs=[
            pl.BlockSpec((gb, ATTN_WIDTH), row),
            pl.BlockSpec((gb, w, KV_WIDTH), blk3),
            pl.BlockSpec((gb, w, KV_WIDTH), blk3),
        ],
        out_shape=[
            jax.ShapeDtypeStruct((nb, ATTN_WIDTH), BF16),
            jax.ShapeDtypeStruct(cache_k.shape, cache_k.dtype),
            jax.ShapeDtypeStruct(cache_v.shape, cache_v.dtype),
        ],
        scratch_shapes=[
            pltpu.VMEM((N_Q_HEADS * gb, LANES), F32),
            pltpu.VMEM((N_Q_HEADS * gb, LANES), F32),
        ],
        compiler_params=pltpu.CompilerParams(
            dimension_semantics=("arbitrary",), vmem_limit_bytes=VMEM_LIMIT),
        name="dec_attn",
    )(sinks, q, k_new, v_new, cache_k, cache_v)


def _dec_hgrn_body(q_ref, k_ref, i_ref, g_ref, nrm_ref, s_ref, o_ref, sn_ref, os_ref, *, gb):
    dk = REC_DK
    zpad = jnp.zeros((dk - gb, dk), F32)
    for h in range(N_REC_HEADS):
        hs = slice(h * dk, (h + 1) * dk)
        qt = jnp.concatenate([q_ref[:, hs], zpad], axis=0).T
        kt = jnp.concatenate([k_ref[:, hs], zpad], axis=0).T
        for j in range(gb):
            kcol = kt[:, j:j + 1]
            s_new = s_ref[j, h] * (1.0 - kcol) + kcol * i_ref[j:j + 1, hs]
            sn_ref[j, h] = s_new
            os_ref[j:j + 1, hs] = jnp.sum(s_new * qt[:, j:j + 1], axis=0, keepdims=True)
    for h in range(N_REC_HEADS):
        hs = slice(h * dk, (h + 1) * dk)
        oh = os_ref[:, hs]
        oh = oh * lax.rsqrt(jnp.mean(oh * oh, axis=-1, keepdims=True) + EPS) * nrm_ref[h:h + 1, :]
        o_ref[:, hs] = (oh * g_ref[:, hs]).astype(o_ref.dtype)


def _dec_hgrn(q, k, iv, g, nrm, state, *, gb):
    nb, w = q.shape
    row = lambda i: (i, 0)
    blk4 = lambda i: (i, 0, 0, 0)
    return pl.pallas_call(
        functools.partial(_dec_hgrn_body, gb=gb),
        grid=(nb // gb,),
        in_specs=[
            pl.BlockSpec((gb, w), row),
            pl.BlockSpec((gb, w), row),
            pl.BlockSpec((gb, w), row),
            pl.BlockSpec((gb, w), row),
            pl.BlockSpec(nrm.shape, lambda i: (0, 0)),
            pl.BlockSpec((gb,) + state.shape[1:], blk4),
        ],
        out_specs=[
            pl.BlockSpec((gb, w), row),
            pl.BlockSpec((gb,) + state.shape[1:], blk4),
        ],
        out_shape=[
            jax.ShapeDtypeStruct((nb, w), BF16),
            jax.ShapeDtypeStruct(state.shape, state.dtype),
        ],
        scratch_shapes=[pltpu.VMEM((gb, w), F32)],
        compiler_params=pltpu.CompilerParams(
            dimension_semantics=("arbitrary",), vmem_limit_bytes=VMEM_LIMIT),
        name="dec_hgrn",
    )(q, k, iv, g, nrm, state)


def _ffn_body(x_ref, a_ref, o_ref, wo_ref, wg_ref, wu_ref, wd_ref, g1_ref, g2_ref, g3_ref, y_ref):
    _ffn_rows(x_ref, a_ref, o_ref, wo_ref, wg_ref, wu_ref, wd_ref, g1_ref, g2_ref, g3_ref, y_ref,
              slice(0, x_ref.shape[0]))


def _ffn(x, a, o, wo, wg, wu, wd, g1, g2, g3, *, tm):
    n, d = x.shape
    row = lambda i: (i, 0)
    const = lambda i: (0, 0)
    resident = lambda arr: pl.BlockSpec(arr.shape, const, pipeline_mode=pl.Buffered(1))
    return pl.pallas_call(
        _ffn_body,
        grid=(n // tm,),
        in_specs=[
            pl.BlockSpec((tm, d), row),
            pl.BlockSpec((tm, a.shape[1]), row),
            pl.BlockSpec((tm, o.shape[1]), row),
            resident(wo), resident(wg), resident(wu), resident(wd),
            pl.BlockSpec((1, d), const), pl.BlockSpec((1, d), const), pl.BlockSpec((1, d), const),
        ],
        out_specs=pl.BlockSpec((tm, d), row),
        out_shape=jax.ShapeDtypeStruct((n, d), x.dtype),
        compiler_params=pltpu.CompilerParams(
            dimension_semantics=("arbitrary",), vmem_limit_bytes=VMEM_LIMIT),
        name="ffn",
    )(x, a, o, wo, wg, wu, wd, g1, g2, g3)


def _rope_tables(pos):
    half = ROT_DIM // 2
    inv_freq = jnp.exp(-jnp.log(jnp.asarray(ROPE_THETA, F32)) * jnp.arange(half, dtype=F32) * (2.0 / ROT_DIM))
    d = np.arange(LANES) % HEAD_DIM
    rot = d < ROT_DIM
    freq = jnp.where(rot, inv_freq[d % half], 0.0)
    sign = jnp.asarray(np.where(d < half, -1.0, np.where(rot, 1.0, 0.0)), F32)
    ang = pos[:, None] * freq[None, :]
    return jnp.cos(ang), jnp.sin(ang) * sign[None, :]


def _pick_tile(n, want):
    t = min(n, want)
    while n % t:
        t //= 2
    return t


def kernel(x_prompt, x_sample, cache_k_win, cache_v_win, state_hgrn, w_in, w_out, w_gate, w_up, w_down,
           norm_pre_mix, norm_post_mix, norm_pre_ffn, norm_post_ffn, attn_sinks, rec_lb, rec_out_norm):
    depth = w_in.shape[0]
    assert depth == 1, "single-layer step"
    bsz, seq, d = x_prompt.shape
    nb, dec_seq, _ = x_sample.shape
    assert dec_seq == 1 and seq % WINDOW == 0
    w_keep = cache_k_win.shape[2]
    assert w_keep == WINDOW

    w_in_b = w_in[0].astype(BF16)
    w_out_b = w_out[0].astype(BF16)
    w_gate_b = w_gate[0].astype(BF16)
    w_up_b = w_up[0].astype(BF16)
    w_down_b = w_down[0].astype(BF16)
    g_pre, g_post, g_pre_f, g_post_f = norm_pre_mix, norm_post_mix, norm_pre_ffn, norm_post_ffn
    sinks = attn_sinks[0].astype(F32)
    nrm = rec_out_norm[0].astype(F32)

    tm = _pick_tile(seq, 512)
    cos_p, sin_p = _rope_tables(jnp.arange(seq, dtype=F32))
    xp = x_prompt.reshape(bsz * seq, d)
    q, k, v, rq, lf, rk, ri, rg = _inproj(xp, g_pre, w_in_b, cos_p, sin_p, rec_lb, tm=tm, act_dtype=BF16)
    r3 = lambda z: z.reshape(bsz, seq, z.shape[-1])
    k3, v3 = r3(k), r3(v)
    yp, s_fin = _mixffn(xp, q, k, v, rq, lf, rk, ri, rg, sinks, nrm, w_out_b, w_gate_b, w_up_b, w_down_b,
                        g_post, g_pre_f, g_post_f, bsz=bsz, tm=tm, chunk=128)
    y_prompt = yp.reshape(bsz, seq, d)
    keep = min(WINDOW, seq)
    new_k_p = k3[:, seq - keep:].reshape(1, bsz, keep, N_KV_HEADS, HEAD_DIM).astype(cache_k_win.dtype)
    new_v_p = v3[:, seq - keep:].reshape(1, bsz, keep, N_KV_HEADS, HEAD_DIM).astype(cache_v_win.dtype)
    new_s_p = s_fin[None].astype(state_hgrn.dtype)

    cos_s, sin_s = _rope_tables(jnp.full((nb,), float(PAST_LEN), F32))
    xs = x_sample.reshape(nb, d)
    qs, ks, vs, rqs, _, rks, ris, rgs = _inproj(xs, g_pre, w_in_b, cos_s, sin_s, rec_lb, tm=nb, act_dtype=F32)
    gb = 8
    ck = cache_k_win[0].reshape(nb, w_keep, KV_WIDTH)
    cv = cache_v_win[0].reshape(nb, w_keep, KV_WIDTH)
    a_s, nk, nv = _dec_attn(qs, ks, vs, ck, cv, sinks, gb=gb)
    o_s, s_new = _dec_hgrn(rqs, rks, ris, rgs, nrm, state_hgrn[0], gb=gb)
    ys = _ffn(xs, a_s, o_s, w_out_b, w_gate_b, w_up_b, w_down_b, g_post, g_pre_f, g_post_f, tm=nb)
    y_sample = ys.reshape(nb, 1, d)
    new_k_s = nk.reshape(1, nb, w_keep, N_KV_HEADS, HEAD_DIM)
    new_v_s = nv.reshape(1, nb, w_keep, N_KV_HEADS, HEAD_DIM)
    new_s_s = s_new[None]

    return (y_prompt, y_sample, new_k_p, new_v_p, new_s_p, new_k_s, new_v_s, new_s_s)
```

```python
import functools

import jax
import jax.numpy as jnp
import numpy as np
from jax import lax
from jax.experimental import pallas as pl
from jax.experimental.pallas import tpu as pltpu

F32 = jnp.float32
BF16 = jnp.bfloat16

PAST_LEN = 16384
WINDOW = 128
HEAD_DIM = 64
N_Q_HEADS = 8
N_KV_HEADS = 2
ROT_DIM = HEAD_DIM // 4
ROPE_THETA = 500000.0
N_REC_HEADS = 4
REC_DK = 128
EPS = 1e-6

ATTN_WIDTH = N_Q_HEADS * HEAD_DIM
KV_WIDTH = N_KV_HEADS * HEAD_DIM
REC_WIDTH = N_REC_HEADS * REC_DK
LANES = 128
NEG = -1e30

VMEM_LIMIT = 56 * 1024 * 1024


def _dot(a, b):
    return jnp.dot(a, b, preferred_element_type=F32)


def _dot_nt(a, b):
    return lax.dot_general(a, b, (((1,), (1,)), ((), ())), preferred_element_type=F32)


def _dot_tn(a, b):
    return lax.dot_general(a, b, (((0,), (0,)), ((), ())), preferred_element_type=F32)


def _sigmoid(x):
    return 1.0 / (1.0 + jnp.exp(-x))


def _rms(x, g):
    return x * lax.rsqrt(jnp.mean(x * x, axis=-1, keepdims=True) + EPS) * g


def _inproj_body(x_ref, g_ref, w_ref, cos_ref, sin_ref, lbp_ref,
                 q_ref, k_ref, v_ref, rq_ref, lf_ref, rk_ref, ri_ref, rg_ref):
    h = _rms(x_ref[...], g_ref[...]).astype(BF16)
    cos = cos_ref[...]
    sin = sin_ref[...]
    lane = lax.broadcasted_iota(jnp.int32, cos.shape, 1)
    first = (lane & (HEAD_DIM - 1)) < (ROT_DIM // 2)

    def rope(z):
        partner = jnp.where(first, pltpu.roll(z, LANES - ROT_DIM // 2, 1), pltpu.roll(z, ROT_DIM // 2, 1))
        return z * cos + partner * sin

    def proj(lo, width):
        return _dot(h, w_ref[:, lo:lo + width])

    scale = HEAD_DIM ** -0.5
    zq = proj(0, ATTN_WIDTH)
    for j in range(ATTN_WIDTH // LANES):
        z = rope(zq[:, j * LANES:(j + 1) * LANES])
        q_ref[:, j * LANES:(j + 1) * LANES] = (z * scale).astype(q_ref.dtype)
    off = ATTN_WIDTH
    zkv = proj(off, 2 * KV_WIDTH)
    k_ref[...] = rope(zkv[:, :KV_WIDTH])
    v_ref[...] = zkv[:, KV_WIDTH:]
    off += 2 * KV_WIDTH

    z = proj(off, REC_WIDTH)
    rq_ref[...] = (z * _sigmoid(z)).astype(rq_ref.dtype)
    off += REC_WIDTH

    r = lbp_ref[...]
    e = jnp.exp(r - jnp.max(r, axis=0, keepdims=True))
    lb = e[0:1, :] / jnp.sum(e, axis=0, keepdims=True)
    z = proj(off, REC_WIDTH)
    f = lb + (1.0 - lb) * _sigmoid(z)
    lf_ref[...] = jnp.log(f)
    rk_ref[...] = (1.0 - f).astype(rk_ref.dtype)
    off += REC_WIDTH

    ri_ref[...] = proj(off, REC_WIDTH).astype(ri_ref.dtype)
    off += REC_WIDTH
    z = proj(off, REC_WIDTH)
    rg_ref[...] = (z * _sigmoid(z)).astype(rg_ref.dtype)


def _inproj(x, g, w_bf, cos, sin, rec_lb, *, tm, act_dtype):
    n, d = x.shape
    t_tab = cos.shape[0]
    n_tab = t_tab // tm
    in_w = w_bf.shape[1]
    row = lambda i: (i, 0)
    const = lambda i: (0, 0)
    outs = [
        jax.ShapeDtypeStruct((n, ATTN_WIDTH), act_dtype),
        jax.ShapeDtypeStruct((n, KV_WIDTH), F32),
        jax.ShapeDtypeStruct((n, KV_WIDTH), F32),
        jax.ShapeDtypeStruct((n, REC_WIDTH), act_dtype),
        jax.ShapeDtypeStruct((n, REC_WIDTH), F32),
        jax.ShapeDtypeStruct((n, REC_WIDTH), act_dtype),
        jax.ShapeDtypeStruct((n, REC_WIDTH), act_dtype),
        jax.ShapeDtypeStruct((n, REC_WIDTH), act_dtype),
    ]
    return pl.pallas_call(
        _inproj_body,
        grid=(n // tm,),
        in_specs=[
            pl.BlockSpec((tm, d), row),
            pl.BlockSpec((1, d), const),
            pl.BlockSpec((d, in_w), const),
            pl.BlockSpec((tm, LANES), lambda i: (i % n_tab, 0)),
            pl.BlockSpec((tm, LANES), lambda i: (i % n_tab, 0)),
            pl.BlockSpec(rec_lb.shape, const),
        ],
        out_specs=[pl.BlockSpec((tm, o.shape[1]), row) for o in outs],
        out_shape=outs,
        compiler_params=pltpu.CompilerParams(
            dimension_semantics=("arbitrary",), vmem_limit_bytes=VMEM_LIMIT),
        name="inproj",
    )(x, g, w_bf, cos, sin, rec_lb)


def _head_variants(z):
    lane = lax.broadcasted_iota(jnp.int32, z.shape, 1)
    low = lane < HEAD_DIM
    z0 = jnp.where(low, z, 0.0)
    z1 = jnp.where(low, 0.0, z)
    return {
        (0, 0): z0.astype(BF16),
        (1, 1): z1.astype(BF16),
        (0, 1): pltpu.roll(z0, HEAD_DIM, 1).astype(BF16),
        (1, 0): pltpu.roll(z1, HEAD_DIM, 1).astype(BF16),
    }


def _swa_steps(nblk):
    return (nblk + 1) + nblk * N_KV_HEADS * (1 + (N_Q_HEADS // N_KV_HEADS // 2) * 3)


def _swa_tile(sink_ref, q_ref, kc_ref, vc_ref, kp_ref, vp_ref, a_ref, has_prev, nblk):
    w = WINDOW
    key = lax.broadcasted_iota(jnp.int32, (w, w), 0)
    qry = lax.broadcasted_iota(jnp.int32, (w, w), 1)
    cur_ok = key <= qry
    prev_band = key > qry

    kvar, vt = [], []
    for j in range(-1, nblk):
        kblk = kp_ref[...] if j < 0 else kc_ref[j * w:(j + 1) * w, :]
        vblk = vp_ref[...] if j < 0 else vc_ref[j * w:(j + 1) * w, :]
        kvar.append(_head_variants(kblk))
        vt.append(vblk.T.astype(BF16))
        yield

    group = N_Q_HEADS // N_KV_HEADS
    for j in range(nblk):
        prev_ok = prev_band if j > 0 else jnp.logical_and(prev_band, has_prev)
        for h in range(N_KV_HEADS):
            lhs = jnp.concatenate([kvar[j][(h, 0)], kvar[j + 1][(h, 0)],
                                   kvar[j][(h, 1)], kvar[j + 1][(h, 1)]], axis=0)
            vth = jnp.concatenate([vt[j][h * HEAD_DIM:(h + 1) * HEAD_DIM, :],
                                   vt[j + 1][h * HEAD_DIM:(h + 1) * HEAD_DIM, :]], axis=1)
            c0 = h * (group // 2)
            q2 = jnp.concatenate([q_ref[j * w:(j + 1) * w, (c0 + cc) * LANES:(c0 + cc + 1) * LANES]
                                  for cc in range(group // 2)], axis=0)
            st = _dot_nt(lhs, q2)
            yield
            for cc in range(group // 2):
                c = c0 + cc
                qcols = slice(cc * w, (cc + 1) * w)
                ps, rinv = [], []
                for half in range(2):
                    sink = sink_ref[2 * c + half]
                    base = half * 2 * w
                    sp = jnp.where(prev_ok, st[base:base + w, qcols], NEG)
                    sc = jnp.where(cur_ok, st[base + w:base + 2 * w, qcols], NEG)
                    m = jnp.maximum(jnp.max(sp, axis=0, keepdims=True), jnp.max(sc, axis=0, keepdims=True))
                    m = jnp.maximum(m, sink)
                    pp = jnp.exp(sp - m)
                    pc = jnp.exp(sc - m)
                    den = (jnp.sum(pp, axis=0, keepdims=True) + jnp.sum(pc, axis=0, keepdims=True)
                           + jnp.exp(sink - m))
                    ps.append(jnp.concatenate([pp, pc], axis=0).astype(BF16))
                    rinv.append(1.0 / den)
                    yield
                ot2 = _dot(vth, jnp.concatenate(ps, axis=1))
                ot = jnp.concatenate([ot2[:, :w] * rinv[0], ot2[:, w:] * rinv[1]], axis=0)
                a_ref[j * w:(j + 1) * w, c * LANES:(c + 1) * LANES] = ot.T.astype(a_ref.dtype)
                yield


def _split3(x):
    hi = x.astype(BF16)
    r1 = x - hi.astype(F32)
    mid = r1.astype(BF16)
    lo = (r1 - mid.astype(F32)).astype(BF16)
    return hi, mid, lo


def _level_table(c):
    t = np.arange(c)[:, None]
    s = np.arange(c)[None, :]
    x = t ^ s
    lvl = np.where(x > 0, np.floor(np.log2(np.maximum(x, 1))).astype(np.int32), 0)
    nlev = int(np.log2(c))
    return np.where(t > s, lvl, np.where(t == s, nlev, -1)).astype(np.int32)


HEAD_PAIR = 2 * REC_DK


def _hgrn_steps(chunk, nchunk):
    return nchunk * (1 + (N_REC_HEADS // 2) * (chunk // 32 + 3))


def _block_diag(a):
    z = jnp.zeros_like(a[:, :REC_DK])
    return jnp.concatenate([jnp.concatenate([a[:, :REC_DK], z], axis=1),
                            jnp.concatenate([z, a[:, REC_DK:]], axis=1)], axis=0)


def _hgrn_tile(q_ref, lf_ref, k_ref, v_ref, g_ref, nrm_ref, lvl_ref, o_ref,
               st_ref, b_ref, q32_ref, k32_ref, chunk, nchunk):
    c = chunk
    sub = 8
    blk = 16
    nlev = c.bit_length() - 1
    first_slab = 4
    row = lax.broadcasted_iota(jnp.int32, (c, c), 0)
    col = lax.broadcasted_iota(jnp.int32, (c, c), 1)
    tri = (col <= row).astype(BF16)
    tri2 = jnp.concatenate([tri, tri], axis=1)
    brow_i = lax.broadcasted_iota(jnp.int32, (blk, HEAD_PAIR), 0)
    odd = (brow_i & 1) != 0
    side1 = (brow_i & 2) != 0
    side2 = (brow_i & 4) != 0
    side3 = (brow_i & 8) != 0
    low4 = lax.broadcasted_iota(jnp.int32, (sub, HEAD_PAIR), 0) < 4
    lane = lax.broadcasted_iota(jnp.int32, (sub, c), 1)
    cat = lambda parts: parts[0] if len(parts) == 1 else jnp.concatenate(parts, axis=0)

    for ci in range(nchunk):
        sl = slice(ci * c, (ci + 1) * c)
        hi, mid, lo = _split3(lf_ref[sl, :])
        b_ref[...] = _dot(tri2, jnp.concatenate([hi, mid], axis=0)) + _dot(tri, lo)
        q32_ref[...] = q_ref[sl, :].astype(F32)
        k32_ref[...] = k_ref[sl, :].astype(F32)
        lvl = lvl_ref[...]
        yield

        for hp in range(N_REC_HEADS // 2):
            ps = slice(hp * HEAD_PAIR, (hp + 1) * HEAD_PAIR)

            def bcast(r, n=sub):
                return jnp.broadcast_to(b_ref[r:r + 1, ps], (n, HEAD_PAIR))

            blast = b_ref[c - 1:c, ps]
            q_in, k_out = [], []
            lev = [[] for _ in range(nlev)]
            lev_q = [[] for _ in range(nlev)]
            for r0 in range(0, c, blk):
                q = q32_ref[r0:r0 + blk, ps]
                k = k32_ref[r0:r0 + blk, ps]
                b = b_ref[r0:r0 + blk, ps]
                lf = lf_ref[ci * c + r0:ci * c + r0 + blk, ps]
                q_in.append((q * jnp.exp(b)).astype(BF16))
                k_out.append((k * jnp.exp(blast - b)).astype(BF16))
                lev[0].append((jnp.where(odd, q, k) * jnp.exp(jnp.where(odd, lf, 0.0))).astype(BF16))
                bref = jnp.concatenate([jnp.where(low4, bcast(r0 + 1), bcast(r0 + 5)),
                                        jnp.where(low4, bcast(r0 + 9), bcast(r0 + 13))], axis=0)
                lev[1].append((jnp.where(side1, q, k) * jnp.exp(-jnp.abs(b - bref))).astype(BF16))
                bref = jnp.concatenate([bcast(r0 + 3), bcast(r0 + 11)], axis=0)
                lev[2].append((jnp.where(side2, q, k) * jnp.exp(-jnp.abs(b - bref))).astype(BF16))
                bref = bcast(r0 + 7, blk)
                lev[3].append((jnp.where(side3, q, k) * jnp.exp(-jnp.abs(b - bref))).astype(BF16))
                for l in range(first_slab, nlev):
                    m = 1 << l
                    g0 = (r0 // (2 * m)) * (2 * m)
                    bref = bcast(g0 + m - 1, blk)
                    if r0 - g0 < m:
                        lev[l].append((k * jnp.exp(bref - b)).astype(BF16))
                    else:
                        piece = (q * jnp.exp(b - bref)).astype(BF16)
                        lev[l].append(piece)
                        lev_q[l].append(piece)
                if r0 % (2 * blk) == blk:
                    yield
            q_in, k_out = cat(q_in), cat(k_out)

            halves = [slice(0, REC_DK), slice(REC_DK, HEAD_PAIR)]
            pd = _dot_nt(q_ref[sl, ps], _block_diag(k_ref[sl, ps]))
            sc = [jnp.where(lvl == nlev, pd[:, hv], 0.0) for hv in halves]
            for l in range(first_slab):
                a = cat(lev[l])
                p = _dot_nt(a, _block_diag(a))
                sc = [jnp.where(lvl == l, p[:, hv], s) for hv, s in zip(halves, sc)]
            yield
            blocks = [[s[r0:r0 + sub, :] for r0 in range(0, c, sub)] for s in sc]
            for l in range(first_slab, nlev):
                m = 1 << l
                pq = _dot_nt(cat(lev_q[l]), _block_diag(cat(lev[l])))
                for gi, g0 in enumerate(range(0, c, 2 * m)):
                    in_group = jnp.logical_and(lane >= g0, lane < g0 + m)
                    for r in range(0, m, sub):
                        bi = (g0 + m + r) // sub
                        for hh, hv in enumerate(halves):
                            blocks[hh][bi] = jnp.where(in_group, pq[gi * m + r:gi * m + r + sub, hv], blocks[hh][bi])
            sc2 = jnp.concatenate([jnp.concatenate(bl, axis=0) for bl in blocks], axis=1).astype(BF16)
            yield
            st2 = jnp.concatenate([st_ref[2 * hp], st_ref[2 * hp + 1]], axis=1)
            oh2 = (_dot_nt(q_in, _block_diag(st2.astype(BF16)))
                   + _dot(sc2, _block_diag(v_ref[sl, ps])))
            for hh, hv in enumerate(halves):
                h = 2 * hp + hh
                hs = slice(h * REC_DK, (h + 1) * REC_DK)
                st_ref[h] = st2[:, hv] * jnp.exp(b_ref[c - 1:c, hs]) + _dot_tn(v_ref[sl, hs], k_out[:, hv])
                oh = oh2[:, hv]
                oh = oh * lax.rsqrt(jnp.mean(oh * oh, axis=-1, keepdims=True) + EPS) * nrm_ref[h:h + 1, :]
                o_ref[sl, hs] = (oh * g_ref[sl, hs].astype(F32)).astype(o_ref.dtype)
            yield


def _ffn_rows(x_ref, a_ref, o_ref, wo_ref, wg_ref, wu_ref, wd_ref, g1_ref, g2_ref, g3_ref, y_ref, rows):
    aw = a_ref.shape[1]
    mix = _dot(a_ref[rows, :], wo_ref[0:aw, :]) + _dot(o_ref[rows, :], wo_ref[aw:, :])
    x1 = x_ref[rows, :] + _rms(mix, g1_ref[...])
    h2 = _rms(x1, g2_ref[...]).astype(BF16)
    gate = _dot(h2, wg_ref[...])
    up = _dot(h2, wu_ref[...])
    act = (gate * _sigmoid(gate) * up).astype(BF16)
    ffn = _dot(act, wd_ref[...])
    y_ref[rows, :] = x1 + _rms(ffn, g3_ref[...])


FFN_COLS = 256
NORM_ROWS = 128


def _ffn_steps(tm, d, d_ff):
    return 1 + 2 * (tm // NORM_ROWS) + 2 * (d_ff // FFN_COLS) + d // FFN_COLS


def _ffn_tile(x_ref, mix_ref, wo_ref, wg_ref, wu_ref, wd_ref, g1_ref, g2_ref, g3_ref, y_ref,
              x1_ref, h2_ref, act_ref, acc_ref):
    tm, d = x_ref.shape
    d_ff = wg_ref.shape[1]
    mix = _dot(mix_ref[...], wo_ref[...])
    yield
    for r0 in range(0, tm, NORM_ROWS):
        rows = slice(r0, r0 + NORM_ROWS)
        x1 = x_ref[rows, :] + _rms(mix[rows, :], g1_ref[...])
        x1_ref[rows, :] = x1
        h2_ref[rows, :] = _rms(x1, g2_ref[...]).astype(BF16)
        yield
    for c0 in range(0, d_ff, FFN_COLS):
        cols = slice(c0, c0 + FFN_COLS)
        gate = _dot(h2_ref[...], wg_ref[:, cols])
        yield
        up = _dot(h2_ref[...], wu_ref[:, cols])
        act_ref[:, cols] = (gate * _sigmoid(gate) * up).astype(BF16)
        yield
    for c0 in range(0, d, FFN_COLS):
        cols = slice(c0, c0 + FFN_COLS)
        acc_ref[:, cols] = _dot(act_ref[...], wd_ref[:, cols])
        yield
    for r0 in range(0, tm, NORM_ROWS):
        rows = slice(r0, r0 + NORM_ROWS)
        y_ref[rows, :] = x1_ref[rows, :] + _rms(acc_ref[rows, :], g3_ref[...])
        yield


def _interleave(*streams):
    live = [[0, n, gen] for gen, n in streams]
    while live:
        cur = min(live, key=lambda e: e[0] / e[1])
        try:
            next(cur[2])
            cur[0] += 1
        except StopIteration:
            live.remove(cur)


def _mixffn_body(sink_ref, q_ref, kc_ref, vc_ref, kp_ref, vp_ref, rq_ref, lf_ref, rk_ref, ri_ref, rg_ref,
                 nrm_ref, lvl_ref, x_ref, wo_ref, wg_ref, wu_ref, wd_ref, g1_ref, g2_ref, g3_ref,
                 y_ref, sfin_ref, mix_scr, x1_scr, h2_scr, act_scr, acc_scr, st_ref, b_ref, q32_ref, k32_ref,
                 *, tiles_per_seq, ntiles, nblk, chunk, nchunk):
    g = pl.program_id(0)
    real = g < ntiles
    t_in = jnp.minimum(g, ntiles - 1) % tiles_per_seq

    @pl.when(g == 0)
    def _():
        mix_scr[...] = jnp.zeros_like(mix_scr)

    @pl.when(jnp.logical_and(real, t_in == 0))
    def _():
        st_ref[...] = jnp.zeros_like(st_ref)

    tm, d = x_ref.shape
    _interleave(
        (_ffn_tile(x_ref, mix_scr, wo_ref, wg_ref, wu_ref, wd_ref, g1_ref, g2_ref, g3_ref, y_ref,
                   x1_scr, h2_scr, act_scr, acc_scr), _ffn_steps(tm, d, wg_ref.shape[1])),
        (_swa_tile(sink_ref, q_ref, kc_ref, vc_ref, kp_ref, vp_ref, mix_scr.at[:, 0:ATTN_WIDTH], t_in > 0, nblk),
         _swa_steps(nblk)),
        (_hgrn_tile(rq_ref, lf_ref, rk_ref, ri_ref, rg_ref, nrm_ref, lvl_ref, mix_scr.at[:, ATTN_WIDTH:],
                    st_ref, b_ref, q32_ref, k32_ref, chunk, nchunk), _hgrn_steps(chunk, nchunk)),
    )

    @pl.when(jnp.logical_and(real, t_in == tiles_per_seq - 1))
    def _():
        for h in range(N_REC_HEADS):
            sfin_ref[h] = st_ref[h].T


def _mixffn(x, q, k, v, rq, lf, rk, ri, rg, sinks, nrm, wo, wg, wu, wd, g1, g2, g3, *, bsz, tm, chunk):
    n, d = x.shape
    ntiles = n // tm
    tiles_per_seq = ntiles // bsz
    nblk = tm // WINDOW
    w = rq.shape[1]
    lvl = jnp.asarray(_level_table(chunk))
    mix = lambda g: (jnp.minimum(g, ntiles - 1), 0)
    prev = lambda g: (jnp.maximum(jnp.minimum(g, ntiles - 1) * nblk - 1, 0), 0)
    ffn = lambda g: (jnp.maximum(g - 1, 0), 0)
    const = lambda g: (0, 0)
    resident = lambda arr: pl.BlockSpec(arr.shape, const, pipeline_mode=pl.Buffered(1))
    return pl.pallas_call(
        functools.partial(_mixffn_body, tiles_per_seq=tiles_per_seq, ntiles=ntiles, nblk=nblk,
                          chunk=chunk, nchunk=tm // chunk),
        grid=(ntiles + 1,),
        in_specs=[
            pl.BlockSpec(memory_space=pltpu.SMEM),
            pl.BlockSpec((tm, ATTN_WIDTH), mix),
            pl.BlockSpec((tm, KV_WIDTH), mix),
            pl.BlockSpec((tm, KV_WIDTH), mix),
            pl.BlockSpec((WINDOW, KV_WIDTH), prev),
            pl.BlockSpec((WINDOW, KV_WIDTH), prev),
            pl.BlockSpec((tm, w), mix),
            pl.BlockSpec((tm, w), mix),
            pl.BlockSpec((tm, w), mix),
            pl.BlockSpec((tm, w), mix),
            pl.BlockSpec((tm, w), mix),
            pl.BlockSpec(nrm.shape, const),
            pl.BlockSpec((chunk, chunk), const),
            pl.BlockSpec((tm, d), ffn),
            resident(wo), resident(wg), resident(wu), resident(wd),
            pl.BlockSpec((1, d), const), pl.BlockSpec((1, d), const), pl.BlockSpec((1, d), const),
        ],
        out_specs=[
            pl.BlockSpec((tm, d), ffn),
            pl.BlockSpec((None, N_REC_HEADS, REC_DK, REC_DK),
                         lambda g: (jnp.minimum(g, ntiles - 1) // tiles_per_seq, 0, 0, 0)),
        ],
        out_shape=[
            jax.ShapeDtypeStruct((n, d), x.dtype),
            jax.ShapeDtypeStruct((bsz, N_REC_HEADS, REC_DK, REC_DK), F32),
        ],
        scratch_shapes=[
            pltpu.VMEM((tm, ATTN_WIDTH + w), BF16),
            pltpu.VMEM((tm, d), F32),
            pltpu.VMEM((tm, d), BF16),
            pltpu.VMEM((tm, wg.shape[1]), BF16),
            pltpu.VMEM((tm, d), F32),
            pltpu.VMEM((N_REC_HEADS, REC_DK, REC_DK), F32),
            pltpu.VMEM((chunk, w), F32),
            pltpu.VMEM((chunk, w), F32),
            pltpu.VMEM((chunk, w), F32),
        ],
        compiler_params=pltpu.CompilerParams(
            dimension_semantics=("arbitrary",), vmem_limit_bytes=VMEM_LIMIT),
        name="mixffn",
    )(sinks, q, k, v, k, v, rq, lf, rk, ri, rg, nrm, lvl, x, wo, wg, wu, wd, g1, g2, g3)


def _dec_attn_body(sink_ref, q_ref, kn_ref, vn_ref, ck_ref, cv_ref, a_ref, nk_ref, nv_ref,
                   qs_ref, os_ref, *, gb):
    group = N_Q_HEADS // N_KV_HEADS
    lane = lax.broadcasted_iota(jnp.int32, (gb, LANES), 1)
    low = lane < HEAD_DIM
    q = q_ref[...]
    for hq in range(N_Q_HEADS):
        z = q[:, (hq // 2) * LANES:(hq // 2 + 1) * LANES]
        z = jnp.where(low, z, 0.0) if hq % 2 == 0 else jnp.where(low, 0.0, z)
        if hq % 2 != hq // group:
            z = pltpu.roll(z, HEAD_DIM, 1)
        qs_ref[hq * gb:(hq + 1) * gb, :] = z

    hrow = lax.broadcasted_iota(jnp.int32, (N_Q_HEADS, 1), 0)
    sink = jnp.zeros((N_Q_HEADS, 1), F32)
    for hq in range(N_Q_HEADS):
        sink = jnp.where(hrow == hq, sink_ref[hq], sink)
    pad = jnp.zeros((8, LANES), BF16)
    w = ck_ref.shape[1]
    last = lax.broadcasted_iota(jnp.int32, (w, LANES), 0) == w - 1

    for j in range(gb):
        kk = jnp.where(last, kn_ref[j:j + 1, :], pltpu.roll(ck_ref[j], w - 1, 0))
        vv = jnp.where(last, vn_ref[j:j + 1, :], pltpu.roll(cv_ref[j], w - 1, 0))
        nk_ref[j] = kk
        nv_ref[j] = vv
        qp = qs_ref[pl.ds(j, N_Q_HEADS, stride=gb), :]
        qp = jnp.concatenate([qp.astype(BF16), pad], axis=0)
        s = _dot_nt(qp, kk.astype(BF16))[0:N_Q_HEADS, :]
        m = jnp.maximum(jnp.max(s, axis=-1, keepdims=True), sink)
        p = jnp.exp(s - m)
        den = jnp.sum(p, axis=-1, keepdims=True) + jnp.exp(sink - m)
        pb = jnp.concatenate([p.astype(BF16), pad], axis=0)
        o = _dot(pb, vv.astype(BF16))[0:N_Q_HEADS, :] / den
        os_ref[j * N_Q_HEADS:(j + 1) * N_Q_HEADS, :] = o

    for cidx in range(ATTN_WIDTH // LANES):
        parts = []
        for half in range(2):
            hq = 2 * cidx + half
            z = os_ref[pl.ds(hq, gb, stride=N_Q_HEADS), :]
            if half != hq // group:
                z = pltpu.roll(z, HEAD_DIM, 1)
            parts.append(z)
        a_ref[:, cidx * LANES:(cidx + 1) * LANES] = jnp.where(low, parts[0], parts[1]).astype(a_ref.dtype)


def _dec_attn(q, k_new, v_new, cache_k, cache_v, sinks, *, gb):
    nb = q.shape[0]
    w = cache_k.shape[1]
    row = lambda i: (i, 0)
    blk3 = lambda i: (i, 0, 0)
    return pl.pallas_call(
        functools.partial(_dec_attn_body, gb=gb),
        grid=(nb // gb,),
        in_specs=[
            pl.BlockSpec(memory_space=pltpu.SMEM),
            pl.BlockSpec((gb, ATTN_WIDTH), row),
            pl.BlockSpec((gb, KV_WIDTH), row),
            pl.BlockSpec((gb, KV_WIDTH), row),
            pl.BlockSpec((gb, w, KV_WIDTH), blk3),
            pl.BlockSpec((gb, w, KV_WIDTH), blk3),
        ],
        out_specs=[
            pl.BlockSpec((gb, ATTN_WIDTH), row),
            pl.BlockSpec((gb, w, KV_WIDTH), blk3),
            pl.BlockSpec((gb, w, KV_WIDTH), blk3),
        ],
        out_shape=[
            jax.ShapeDtypeStruct((nb, ATTN_WIDTH), BF16),
            jax.ShapeDtypeStruct(cache_k.shape, cache_k.dtype),
            jax.ShapeDtypeStruct(cache_v.shape, cache_v.dtype),
        ],
        scratch_shapes=[
            pltpu.VMEM((N_Q_HEADS * gb, LANES), F32),
            pltpu.VMEM((N_Q_HEADS * gb, LANES), F32),
        ],
        compiler_params=pltpu.CompilerParams(
            dimension_semantics=("arbitrary",), vmem_limit_bytes=VMEM_LIMIT),
        name="dec_attn",
    )(sinks, q, k_new, v_new, cache_k, cache_v)


def _dec_hgrn_body(q_ref, k_ref, i_ref, g_ref, nrm_ref, s_ref, o_ref, sn_ref, os_ref, *, gb):
    dk = REC_DK
    zpad = jnp.zeros((dk - gb, dk), F32)
    for h in range(N_REC_HEADS):
        hs = slice(h * dk, (h + 1) * dk)
        qt = jnp.concatenate([q_ref[:, hs], zpad], axis=0).T
        kt = jnp.concatenate([k_ref[:, hs], zpad], axis=0).T
        for j in range(gb):
            kcol = kt[:, j:j + 1]
            s_new = s_ref[j, h] * (1.0 - kcol) + kcol * i_ref[j:j + 1, hs]
            sn_ref[j, h] = s_new
            os_ref[j:j + 1, hs] = jnp.sum(s_new * qt[:, j:j + 1], axis=0, keepdims=True)
    for h in range(N_REC_HEADS):
        hs = slice(h * dk, (h + 1) * dk)
        oh = os_ref[:, hs]
        oh = oh * lax.rsqrt(jnp.mean(oh * oh, axis=-1, keepdims=True) + EPS) * nrm_ref[h:h + 1, :]
        o_ref[:, hs] = (oh * g_ref[:, hs]).astype(o_ref.dtype)


def _dec_hgrn(q, k, iv, g, nrm, state, *, gb):
    nb, w = q.shape
    row = lambda i: (i, 0)
    blk4 = lambda i: (i, 0, 0, 0)
    return pl.pallas_call(
        functools.partial(_dec_hgrn_body, gb=gb),
        grid=(nb // gb,),
        in_specs=[
            pl.BlockSpec((gb, w), row),
            pl.BlockSpec((gb, w), row),
            pl.BlockSpec((gb, w), row),
            pl.BlockSpec((gb, w), row),
            pl.BlockSpec(nrm.shape, lambda i: (0, 0)),
            pl.BlockSpec((gb,) + state.shape[1:], blk4),
        ],
        out_specs=[
            pl.BlockSpec((gb, w), row),
            pl.BlockSpec((gb,) + state.shape[1:], blk4),
        ],
        out_shape=[
            jax.ShapeDtypeStruct((nb, w), BF16),
            jax.ShapeDtypeStruct(state.shape, state.dtype),
        ],
        scratch_shapes=[pltpu.VMEM((gb, w), F32)],
        compiler_params=pltpu.CompilerParams(
            dimension_semantics=("arbitrary",), vmem_limit_bytes=VMEM_LIMIT),
        name="dec_hgrn",
    )(q, k, iv, g, nrm, state)


def _ffn_body(x_ref, a_ref, o_ref, wo_ref, wg_ref, wu_ref, wd_ref, g1_ref, g2_ref, g3_ref, y_ref):
    _ffn_rows(x_ref, a_ref, o_ref, wo_ref, wg_ref, wu_ref, wd_ref, g1_ref, g2_ref, g3_ref, y_ref,
              slice(0, x_ref.shape[0]))


def _ffn(x, a, o, wo, wg, wu, wd, g1, g2, g3, *, tm):
    n, d = x.shape
    row = lambda i: (i, 0)
    const = lambda i: (0, 0)
    resident = lambda arr: pl.BlockSpec(arr.shape, const, pipeline_mode=pl.Buffered(1))
    return pl.pallas_call(
        _ffn_body,
        grid=(n // tm,),
        in_specs=[
            pl.BlockSpec((tm, d), row),
            pl.BlockSpec((tm, a.shape[1]), row),
            pl.BlockSpec((tm, o.shape[1]), row),
            resident(wo), resident(wg), resident(wu), resident(wd),
            pl.BlockSpec((1, d), const), pl.BlockSpec((1, d), const), pl.BlockSpec((1, d), const),
        ],
        out_specs=pl.BlockSpec((tm, d), row),
        out_shape=jax.ShapeDtypeStruct((n, d), x.dtype),
        compiler_params=pltpu.CompilerParams(
            dimension_semantics=("arbitrary",), vmem_limit_bytes=VMEM_LIMIT),
        name="ffn",
    )(x, a, o, wo, wg, wu, wd, g1, g2, g3)


def _rope_tables(pos):
    half = ROT_DIM // 2
    inv_freq = jnp.exp(-jnp.log(jnp.asarray(ROPE_THETA, F32)) * jnp.arange(half, dtype=F32) * (2.0 / ROT_DIM))
    d = np.arange(LANES) % HEAD_DIM
    rot = d < ROT_DIM
    freq = jnp.where(rot, inv_freq[d % half], 0.0)
    sign = jnp.asarray(np.where(d < half, -1.0, np.where(rot, 1.0, 0.0)), F32)
    ang = pos[:, None] * freq[None, :]
    return jnp.cos(ang), jnp.sin(ang) * sign[None, :]


def _pick_tile(n, want):
    t = min(n, want)
    while n % t:
        t //= 2
    return t


def kernel(x_prompt, x_sample, cache_k_win, cache_v_win, state_hgrn, w_in, w_out, w_gate, w_up, w_down,
           norm_pre_mix, norm_post_mix, norm_pre_ffn, norm_post_ffn, attn_sinks, rec_lb, rec_out_norm):
    depth = w_in.shape[0]
    assert depth == 1, "single-layer step"
    bsz, seq, d = x_prompt.shape
    nb, dec_seq, _ = x_sample.shape
    assert dec_seq == 1 and seq % WINDOW == 0
    w_keep = cache_k_win.shape[2]
    assert w_keep == WINDOW

    w_in_b = w_in[0].astype(BF16)
    w_out_b = w_out[0].astype(BF16)
    w_gate_b = w_gate[0].astype(BF16)
    w_up_b = w_up[0].astype(BF16)
    w_down_b = w_down[0].astype(BF16)
    g_pre, g_post, g_pre_f, g_post_f = norm_pre_mix, norm_post_mix, norm_pre_ffn, norm_post_ffn
    sinks = attn_sinks[0].astype(F32)
    nrm = rec_out_norm[0].astype(F32)

    tm = _pick_tile(seq, 512)
    cos_p, sin_p = _rope_tables(jnp.arange(seq, dtype=F32))
    xp = x_prompt.reshape(bsz * seq, d)
    q, k, v, rq, lf, rk, ri, rg = _inproj(xp, g_pre, w_in_b, cos_p, sin_p, rec_lb, tm=tm, act_dtype=BF16)
    r3 = lambda z: z.reshape(bsz, seq, z.shape[-1])
    k3, v3 = r3(k), r3(v)
    yp, s_fin = _mixffn(xp, q, k, v, rq, lf, rk, ri, rg, sinks, nrm, w_out_b, w_gate_b, w_up_b, w_down_b,
                        g_post, g_pre_f, g_post_f, bsz=bsz, tm=tm, chunk=128)
    y_prompt = yp.reshape(bsz, seq, d)
    keep = min(WINDOW, seq)
    new_k_p = k3[:, seq - keep:].reshape(1, bsz, keep, N_KV_HEADS, HEAD_DIM).astype(cache_k_win.dtype)
    new_v_p = v3[:, seq - keep:].reshape(1, bsz, keep, N_KV_HEADS, HEAD_DIM).astype(cache_v_win.dtype)
    new_s_p = s_fin[None].astype(state_hgrn.dtype)

    cos_s, sin_s = _rope_tables(jnp.full((nb,), float(PAST_LEN), F32))
    xs = x_sample.reshape(nb, d)
    qs, ks, vs, rqs, _, rks, ris, rgs = _inproj(xs, g_pre, w_in_b, cos_s, sin_s, rec_lb, tm=nb, act_dtype=F32)
    gb = 8
    ck = cache_k_win[0].reshape(nb, w_keep, KV_WIDTH)
    cv = cache_v_win[0].reshape(nb, w_keep, KV_WIDTH)
    a_s, nk, nv = _dec_attn(qs, ks, vs, ck, cv, sinks, gb=gb)
    o_s, s_new = _dec_hgrn(rqs, rks, ris, rgs, nrm, state_hgrn[0], gb=gb)
    ys = _ffn(xs, a_s, o_s, w_out_b, w_gate_b, w_up_b, w_down_b, g_post, g_pre_f, g_post_f, tm=nb)
    y_sample = ys.reshape(nb, 1, d)
    new_k_s = nk.reshape(1, nb, w_keep, N_KV_HEADS, HEAD_DIM)
    new_v_s = nv.reshape(1, nb, w_keep, N_KV_HEADS, HEAD_DIM)
    new_s_s = s_new[None]

    return (y_prompt, y_sample, new_k_p, new_v_p, new_s_p, new_k_s, new_v_s, new_s_s)
```

```python
import functools

import jax
import jax.numpy as jnp
import numpy as np
from jax import lax
from jax.experimental import pallas as pl
from jax.experimental.pallas import tpu as pltpu

F32 = jnp.float32
BF16 = jnp.bfloat16

PAST_LEN = 16384
WINDOW = 128
HEAD_DIM = 64
N_Q_HEADS = 8
N_KV_HEADS = 2
ROT_DIM = HEAD_DIM // 4
ROPE_THETA = 500000.0
N_REC_HEADS = 4
REC_DK = 128
EPS = 1e-6

ATTN_WIDTH = N_Q_HEADS * HEAD_DIM
KV_WIDTH = N_KV_HEADS * HEAD_DIM
REC_WIDTH = N_REC_HEADS * REC_DK
LANES = 128
NEG = -1e30

VMEM_LIMIT = 56 * 1024 * 1024


def _dot(a, b):
    return jnp.dot(a, b, preferred_element_type=F32)


def _dot_nt(a, b):
    return lax.dot_general(a, b, (((1,), (1,)), ((), ())), preferred_element_type=F32)


def _dot_tn(a, b):
    return lax.dot_general(a, b, (((0,), (0,)), ((), ())), preferred_element_type=F32)


def _sigmoid(x):
    return 1.0 / (1.0 + jnp.exp(-x))


def _rms(x, g):
    return x * lax.rsqrt(jnp.mean(x * x, axis=-1, keepdims=True) + EPS) * g


def _inproj_body(x_ref, g_ref, w_ref, cr_ref, sr_ref, srs_ref, cb_ref, sb_ref, sbs_ref, lbp_ref,
                 q_ref, k_ref, v_ref, rq_ref, lf_ref, rk_ref, ri_ref, rg_ref):
    h = _rms(x_ref[...], g_ref[...]).astype(BF16)
    cr = cr_ref[...]
    cos = cr * cb_ref[...] - sr_ref[...] * sb_ref[...]
    sin = srs_ref[...] * cb_ref[...] + cr * sbs_ref[...]
    lane = lax.broadcasted_iota(jnp.int32, cos.shape, 1)
    first = (lane & (HEAD_DIM - 1)) < (ROT_DIM // 2)

    def rope(z):
        partner = jnp.where(first, pltpu.roll(z, LANES - ROT_DIM // 2, 1), pltpu.roll(z, ROT_DIM // 2, 1))
        return z * cos + partner * sin

    def proj(lo, width):
        return _dot(h, w_ref[:, lo:lo + width])

    scale = HEAD_DIM ** -0.5
    zq = proj(0, ATTN_WIDTH)
    for j in range(ATTN_WIDTH // LANES):
        z = rope(zq[:, j * LANES:(j + 1) * LANES])
        q_ref[:, j * LANES:(j + 1) * LANES] = (z * scale).astype(q_ref.dtype)
    off = ATTN_WIDTH
    zkv = proj(off, 2 * KV_WIDTH)
    k_ref[...] = rope(zkv[:, :KV_WIDTH])
    v_ref[...] = zkv[:, KV_WIDTH:]
    off += 2 * KV_WIDTH

    z = proj(off, REC_WIDTH)
    rq_ref[...] = (z * _sigmoid(z)).astype(rq_ref.dtype)
    off += REC_WIDTH

    r = lbp_ref[...]
    e = jnp.exp(r - jnp.max(r, axis=0, keepdims=True))
    lb = e[0:1, :] / jnp.sum(e, axis=0, keepdims=True)
    z = proj(off, REC_WIDTH)
    f = lb + (1.0 - lb) * _sigmoid(z)
    lf_ref[...] = jnp.log(f)
    rk_ref[...] = (1.0 - f).astype(rk_ref.dtype)
    off += REC_WIDTH

    ri_ref[...] = proj(off, REC_WIDTH).astype(ri_ref.dtype)
    off += REC_WIDTH
    z = proj(off, REC_WIDTH)
    rg_ref[...] = (z * _sigmoid(z)).astype(rg_ref.dtype)


def _inproj(x, g, w_bf, row_tabs, base_tabs, rec_lb, *, tm, act_dtype):
    n, d = x.shape
    n_base = base_tabs[0].shape[0]
    in_w = w_bf.shape[1]
    row = lambda i: (i, 0)
    const = lambda i: (0, 0)
    base = lambda i: (i % n_base, 0, 0)
    outs = [
        jax.ShapeDtypeStruct((n, ATTN_WIDTH), act_dtype),
        jax.ShapeDtypeStruct((n, KV_WIDTH), F32),
        jax.ShapeDtypeStruct((n, KV_WIDTH), F32),
        jax.ShapeDtypeStruct((n, REC_WIDTH), act_dtype),
        jax.ShapeDtypeStruct((n, REC_WIDTH), F32),
        jax.ShapeDtypeStruct((n, REC_WIDTH), act_dtype),
        jax.ShapeDtypeStruct((n, REC_WIDTH), act_dtype),
        jax.ShapeDtypeStruct((n, REC_WIDTH), act_dtype),
    ]
    return pl.pallas_call(
        _inproj_body,
        grid=(n // tm,),
        in_specs=[
            pl.BlockSpec((tm, d), row),
            pl.BlockSpec((1, d), const),
            pl.BlockSpec((d, in_w), const, pipeline_mode=pl.Buffered(1)),
            pl.BlockSpec((tm, LANES), const), pl.BlockSpec((tm, LANES), const), pl.BlockSpec((tm, LANES), const),
            pl.BlockSpec((None, 1, LANES), base), pl.BlockSpec((None, 1, LANES), base),
            pl.BlockSpec((None, 1, LANES), base),
            pl.BlockSpec(rec_lb.shape, const),
        ],
        out_specs=[pl.BlockSpec((tm, o.shape[1]), row) for o in outs],
        out_shape=outs,
        compiler_params=pltpu.CompilerParams(
            dimension_semantics=("arbitrary",), vmem_limit_bytes=VMEM_LIMIT),
        name="inproj",
    )(x, g, w_bf, *row_tabs, *base_tabs, rec_lb)


def _head_variants(z):
    lane = lax.broadcasted_iota(jnp.int32, z.shape, 1)
    low = lane < HEAD_DIM
    z0 = jnp.where(low, z, 0.0)
    z1 = jnp.where(low, 0.0, z)
    return {
        (0, 0): z0.astype(BF16),
        (1, 1): z1.astype(BF16),
        (0, 1): pltpu.roll(z0, HEAD_DIM, 1).astype(BF16),
        (1, 0): pltpu.roll(z1, HEAD_DIM, 1).astype(BF16),
    }


def _swa_steps(nblk):
    return (nblk + 1) + nblk * N_KV_HEADS * (1 + (N_Q_HEADS // N_KV_HEADS // 2) * 3)


def _swa_tile(sink_ref, q_ref, kc_ref, vc_ref, kp_ref, vp_ref, a_ref, has_prev, nblk):
    w = WINDOW
    key = lax.broadcasted_iota(jnp.int32, (w, w), 0)
    qry = lax.broadcasted_iota(jnp.int32, (w, w), 1)
    cur_ok = key <= qry
    prev_band = key > qry

    kvar, vt = [], []
    for j in range(-1, nblk):
        kblk = kp_ref[...] if j < 0 else kc_ref[j * w:(j + 1) * w, :]
        vblk = vp_ref[...] if j < 0 else vc_ref[j * w:(j + 1) * w, :]
        kvar.append(_head_variants(kblk))
        vt.append(vblk.T.astype(BF16))
        yield

    group = N_Q_HEADS // N_KV_HEADS
    for j in range(nblk):
        prev_ok = prev_band if j > 0 else jnp.logical_and(prev_band, has_prev)
        for h in range(N_KV_HEADS):
            lhs = jnp.concatenate([kvar[j][(h, 0)], kvar[j + 1][(h, 0)],
                                   kvar[j][(h, 1)], kvar[j + 1][(h, 1)]], axis=0)
            vth = jnp.concatenate([vt[j][h * HEAD_DIM:(h + 1) * HEAD_DIM, :],
                                   vt[j + 1][h * HEAD_DIM:(h + 1) * HEAD_DIM, :]], axis=1)
            c0 = h * (group // 2)
            q2 = jnp.concatenate([q_ref[j * w:(j + 1) * w, (c0 + cc) * LANES:(c0 + cc + 1) * LANES]
                                  for cc in range(group // 2)], axis=0)
            st = _dot_nt(lhs, q2)
            yield
            for cc in range(group // 2):
                c = c0 + cc
                qcols = slice(cc * w, (cc + 1) * w)
                ps, rinv = [], []
                for half in range(2):
                    sink = sink_ref[2 * c + half]
                    base = half * 2 * w
                    sp = jnp.where(prev_ok, st[base:base + w, qcols], NEG)
                    sc = jnp.where(cur_ok, st[base + w:base + 2 * w, qcols], NEG)
                    m = jnp.maximum(jnp.max(sp, axis=0, keepdims=True), jnp.max(sc, axis=0, keepdims=True))
                    m = jnp.maximum(m, sink)
                    pp = jnp.exp(sp - m)
                    pc = jnp.exp(sc - m)
                    den = (jnp.sum(pp, axis=0, keepdims=True) + jnp.sum(pc, axis=0, keepdims=True)
                           + jnp.exp(sink - m))
                    ps.append(jnp.concatenate([pp, pc], axis=0).astype(BF16))
                    rinv.append(1.0 / den)
                    yield
                ot2 = _dot(vth, jnp.concatenate(ps, axis=1))
                ot = jnp.concatenate([ot2[:, :w] * rinv[0], ot2[:, w:] * rinv[1]], axis=0)
                a_ref[j * w:(j + 1) * w, c * LANES:(c + 1) * LANES] = ot.T.astype(a_ref.dtype)
                yield


def _split3(x):
    hi = x.astype(BF16)
    r1 = x - hi.astype(F32)
    mid = r1.astype(BF16)
    lo = (r1 - mid.astype(F32)).astype(BF16)
    return hi, mid, lo


def _level_table(c):
    t = np.arange(c)[:, None]
    s = np.arange(c)[None, :]
    x = t ^ s
    lvl = np.where(x > 0, np.floor(np.log2(np.maximum(x, 1))).astype(np.int32), 0)
    nlev = int(np.log2(c))
    return np.where(t > s, lvl, np.where(t == s, nlev, -1)).astype(np.int32)


HEAD_PAIR = 2 * REC_DK


def _hgrn_steps(chunk, nchunk):
    return nchunk * (1 + (N_REC_HEADS // 2) * (chunk // 32 + 3))


def _block_diag(a):
    z = jnp.zeros_like(a[:, :REC_DK])
    return jnp.concatenate([jnp.concatenate([a[:, :REC_DK], z], axis=1),
                            jnp.concatenate([z, a[:, REC_DK:]], axis=1)], axis=0)


def _hgrn_tile(q_ref, lf_ref, k_ref, v_ref, g_ref, nrm_ref, lvl_ref, o_ref,
               st_ref, b_ref, q32_ref, k32_ref, chunk, nchunk):
    c = chunk
    sub = 8
    blk = 16
    nlev = c.bit_length() - 1
    first_slab = 4
    row = lax.broadcasted_iota(jnp.int32, (c, c), 0)
    col = lax.broadcasted_iota(jnp.int32, (c, c), 1)
    tri = (col <= row).astype(BF16)
    tri2 = jnp.concatenate([tri, tri], axis=1)
    brow_i = lax.broadcasted_iota(jnp.int32, (blk, HEAD_PAIR), 0)
    odd = (brow_i & 1) != 0
    side1 = (brow_i & 2) != 0
    side2 = (brow_i & 4) != 0
    side3 = (brow_i & 8) != 0
    low4 = lax.broadcasted_iota(jnp.int32, (sub, HEAD_PAIR), 0) < 4
    lane = lax.broadcasted_iota(jnp.int32, (sub, c), 1)
    cat = lambda parts: parts[0] if len(parts) == 1 else jnp.concatenate(parts, axis=0)

    for ci in range(nchunk):
        sl = slice(ci * c, (ci + 1) * c)
        hi, mid, lo = _split3(lf_ref[sl, :])
        b_ref[...] = _dot(tri2, jnp.concatenate([hi, mid], axis=0)) + _dot(tri, lo)
        q32_ref[...] = q_ref[sl, :].astype(F32)
        k32_ref[...] = k_ref[sl, :].astype(F32)
        lvl = lvl_ref[...]
        yield

        for hp in range(N_REC_HEADS // 2):
            ps = slice(hp * HEAD_PAIR, (hp + 1) * HEAD_PAIR)

            def bcast(r, n=sub):
                return jnp.broadcast_to(b_ref[r:r + 1, ps], (n, HEAD_PAIR))

            blast = b_ref[c - 1:c, ps]
            q_in, k_out = [], []
            lev = [[] for _ in range(nlev)]
            lev_q = [[] for _ in range(nlev)]
            for r0 in range(0, c, blk):
                q = q32_ref[r0:r0 + blk, ps]
                k = k32_ref[r0:r0 + blk, ps]
                b = b_ref[r0:r0 + blk, ps]
                lf = lf_ref[ci * c + r0:ci * c + r0 + blk, ps]
                q_in.append((q * jnp.exp(b)).astype(BF16))
                k_out.append((k * jnp.exp(blast - b)).astype(BF16))
                lev[0].append((jnp.where(odd, q, k) * jnp.exp(jnp.where(odd, lf, 0.0))).astype(BF16))
                bref = jnp.concatenate([jnp.where(low4, bcast(r0 + 1), bcast(r0 + 5)),
                                        jnp.where(low4, bcast(r0 + 9), bcast(r0 + 13))], axis=0)
                lev[1].append((jnp.where(side1, q, k) * jnp.exp(-jnp.abs(b - bref))).astype(BF16))
                bref = jnp.concatenate([bcast(r0 + 3), bcast(r0 + 11)], axis=0)
                lev[2].append((jnp.where(side2, q, k) * jnp.exp(-jnp.abs(b - bref))).astype(BF16))
                bref = bcast(r0 + 7, blk)
                lev[3].append((jnp.where(side3, q, k) * jnp.exp(-jnp.abs(b - bref))).astype(BF16))
                for l in range(first_slab, nlev):
                    m = 1 << l
                    g0 = (r0 // (2 * m)) * (2 * m)
                    bref = bcast(g0 + m - 1, blk)
                    if r0 - g0 < m:
                        lev[l].append((k * jnp.exp(bref - b)).astype(BF16))
                    else:
                        piece = (q * jnp.exp(b - bref)).astype(BF16)
                        lev[l].append(piece)
                        lev_q[l].append(piece)
                if r0 % (2 * blk) == blk:
                    yield
            q_in, k_out = cat(q_in), cat(k_out)

            halves = [slice(0, REC_DK), slice(REC_DK, HEAD_PAIR)]
            pd = _dot_nt(q_ref[sl, ps], _block_diag(k_ref[sl, ps]))
            sc = [jnp.where(lvl == nlev, pd[:, hv], 0.0) for hv in halves]
            for l in range(first_slab):
                a = cat(lev[l])
                p = _dot_nt(a, _block_diag(a))
                sc = [jnp.where(lvl == l, p[:, hv], s) for hv, s in zip(halves, sc)]
            yield
            blocks = [[s[r0:r0 + sub, :] for r0 in range(0, c, sub)] for s in sc]
            for l in range(first_slab, nlev):
                m = 1 << l
                pq = _dot_nt(cat(lev_q[l]), _block_diag(cat(lev[l])))
                for gi, g0 in enumerate(range(0, c, 2 * m)):
                    in_group = jnp.logical_and(lane >= g0, lane < g0 + m)
                    for r in range(0, m, sub):
                        bi = (g0 + m + r) // sub
                        for hh, hv in enumerate(halves):
                            blocks[hh][bi] = jnp.where(in_group, pq[gi * m + r:gi * m + r + sub, hv], blocks[hh][bi])
            sc2 = jnp.concatenate([jnp.concatenate(bl, axis=0) for bl in blocks], axis=1).astype(BF16)
            yield
            st2 = jnp.concatenate([st_ref[2 * hp], st_ref[2 * hp + 1]], axis=1)
            oh2 = (_dot_nt(q_in, _block_diag(st2.astype(BF16)))
                   + _dot(sc2, _block_diag(v_ref[sl, ps])))
            for hh, hv in enumerate(halves):
                h = 2 * hp + hh
                hs = slice(h * REC_DK, (h + 1) * REC_DK)
                st_ref[h] = st2[:, hv] * jnp.exp(b_ref[c - 1:c, hs]) + _dot_tn(v_ref[sl, hs], k_out[:, hv])
                oh = oh2[:, hv]
                oh = oh * lax.rsqrt(jnp.mean(oh * oh, axis=-1, keepdims=True) + EPS) * nrm_ref[h:h + 1, :]
                o_ref[sl, hs] = (oh * g_ref[sl, hs].astype(F32)).astype(o_ref.dtype)
            yield


def _ffn_rows(x_ref, a_ref, o_ref, wo_ref, wg_ref, wu_ref, wd_ref, g1_ref, g2_ref, g3_ref, y_ref, rows):
    aw = a_ref.shape[1]
    mix = _dot(a_ref[rows, :], wo_ref[0:aw, :]) + _dot(o_ref[rows, :], wo_ref[aw:, :])
    x1 = x_ref[rows, :] + _rms(mix, g1_ref[...])
    h2 = _rms(x1, g2_ref[...]).astype(BF16)
    gate = _dot(h2, wg_ref[...])
    up = _dot(h2, wu_ref[...])
    act = (gate * _sigmoid(gate) * up).astype(BF16)
    ffn = _dot(act, wd_ref[...])
    y_ref[rows, :] = x1 + _rms(ffn, g3_ref[...])


FFN_COLS = 256
FFN_ROWS = 256
NORM_ROWS = 128


def _ffn_steps(tm, d, d_ff):
    return 1 + 2 * (tm // NORM_ROWS) + (tm // FFN_ROWS) * (d_ff // FFN_COLS + d // FFN_COLS)


def _ffn_tile(x_ref, mix_ref, wo_ref, wg_ref, wu_ref, wd_ref, g1_ref, g2_ref, g3_ref, y_ref,
              x1_ref, h2_ref, act_ref, acc_ref):
    tm, d = x_ref.shape
    d_ff = wg_ref.shape[1]
    mix = _dot(mix_ref[...], wo_ref[...])
    yield
    for r0 in range(0, tm, NORM_ROWS):
        rows = slice(r0, r0 + NORM_ROWS)
        x1 = x_ref[rows, :] + _rms(mix[rows, :], g1_ref[...])
        x1_ref[rows, :] = x1
        h2_ref[rows, :] = _rms(x1, g2_ref[...]).astype(BF16)
        yield
    for c0 in range(0, d_ff, FFN_COLS):
        cols = slice(c0, c0 + FFN_COLS)
        for r0 in range(0, tm, FFN_ROWS):
            rows = slice(r0, r0 + FFN_ROWS)
            gate = _dot(h2_ref[rows, :], wg_ref[:, cols])
            up = _dot(h2_ref[rows, :], wu_ref[:, cols])
            act_ref[rows, cols] = (gate * _sigmoid(gate) * up).astype(BF16)
            yield
    for c0 in range(0, d, FFN_COLS):
        cols = slice(c0, c0 + FFN_COLS)
        for r0 in range(0, tm, FFN_ROWS):
            rows = slice(r0, r0 + FFN_ROWS)
            acc_ref[rows, cols] = _dot(act_ref[rows, :], wd_ref[:, cols])
            yield
    for r0 in range(0, tm, NORM_ROWS):
        rows = slice(r0, r0 + NORM_ROWS)
        y_ref[rows, :] = x1_ref[rows, :] + _rms(acc_ref[rows, :], g3_ref[...])
        yield


def _interleave(*streams):
    live = [[0, n, gen] for gen, n in streams]
    while live:
        cur = min(live, key=lambda e: e[0] / e[1])
        try:
            next(cur[2])
            cur[0] += 1
        except StopIteration:
            live.remove(cur)


def _mixffn_body(sink_ref, q_ref, kc_ref, vc_ref, kp_ref, vp_ref, rq_ref, lf_ref, rk_ref, ri_ref, rg_ref,
                 nrm_ref, lvl_ref, x_ref, wo_ref, wg_ref, wu_ref, wd_ref, g1_ref, g2_ref, g3_ref,
                 y_ref, sfin_ref, mix_scr, x1_scr, h2_scr, act_scr, acc_scr, st_ref, b_ref, q32_ref, k32_ref,
                 *, tiles_per_seq, ntiles, nblk, chunk, nchunk):
    g = pl.program_id(0)
    real = g < ntiles
    t_in = jnp.minimum(g, ntiles - 1) % tiles_per_seq

    @pl.when(g == 0)
    def _():
        mix_scr[...] = jnp.zeros_like(mix_scr)

    @pl.when(jnp.logical_and(real, t_in == 0))
    def _():
        st_ref[...] = jnp.zeros_like(st_ref)

    tm, d = x_ref.shape
    _interleave(
        (_ffn_tile(x_ref, mix_scr, wo_ref, wg_ref, wu_ref, wd_ref, g1_ref, g2_ref, g3_ref, y_ref,
                   x1_scr, h2_scr, act_scr, acc_scr), _ffn_steps(tm, d, wg_ref.shape[1])),
        (_swa_tile(sink_ref, q_ref, kc_ref, vc_ref, kp_ref, vp_ref, mix_scr.at[:, 0:ATTN_WIDTH], t_in > 0, nblk),
         _swa_steps(nblk)),
        (_hgrn_tile(rq_ref, lf_ref, rk_ref, ri_ref, rg_ref, nrm_ref, lvl_ref, mix_scr.at[:, ATTN_WIDTH:],
                    st_ref, b_ref, q32_ref, k32_ref, chunk, nchunk), _hgrn_steps(chunk, nchunk)),
    )

    @pl.when(jnp.logical_and(real, t_in == tiles_per_seq - 1))
    def _():
        for h in range(N_REC_HEADS):
            sfin_ref[h] = st_ref[h].T


def _mixffn(x, q, k, v, rq, lf, rk, ri, rg, sinks, nrm, wo, wg, wu, wd, g1, g2, g3, *, bsz, tm, chunk):
    n, d = x.shape
    ntiles = n // tm
    tiles_per_seq = ntiles // bsz
    nblk = tm // WINDOW
    w = rq.shape[1]
    lvl = jnp.asarray(_level_table(chunk))
    mix = lambda g: (jnp.minimum(g, ntiles - 1), 0)
    prev = lambda g: (jnp.maximum(jnp.minimum(g, ntiles - 1) * nblk - 1, 0), 0)
    ffn = lambda g: (jnp.maximum(g - 1, 0), 0)
    const = lambda g: (0, 0)
    resident = lambda arr: pl.BlockSpec(arr.shape, const, pipeline_mode=pl.Buffered(1))
    return pl.pallas_call(
        functools.partial(_mixffn_body, tiles_per_seq=tiles_per_seq, ntiles=ntiles, nblk=nblk,
                          chunk=chunk, nchunk=tm // chunk),
        grid=(ntiles + 1,),
        in_specs=[
            pl.BlockSpec(memory_space=pltpu.SMEM),
            pl.BlockSpec((tm, ATTN_WIDTH), mix),
            pl.BlockSpec((tm, KV_WIDTH), mix),
            pl.BlockSpec((tm, KV_WIDTH), mix),
            pl.BlockSpec((WINDOW, KV_WIDTH), prev),
            pl.BlockSpec((WINDOW, KV_WIDTH), prev),
            pl.BlockSpec((tm, w), mix),
            pl.BlockSpec((tm, w), mix),
            pl.BlockSpec((tm, w), mix),
            pl.BlockSpec((tm, w), mix),
            pl.BlockSpec((tm, w), mix),
            pl.BlockSpec(nrm.shape, const),
            pl.BlockSpec((chunk, chunk), const),
            pl.BlockSpec((tm, d), ffn),
            resident(wo), resident(wg), resident(wu), resident(wd),
            pl.BlockSpec((1, d), const), pl.BlockSpec((1, d), const), pl.BlockSpec((1, d), const),
        ],
        out_specs=[
            pl.BlockSpec((tm, d), ffn),
            pl.BlockSpec((None, N_REC_HEADS, REC_DK, REC_DK),
                         lambda g: (jnp.minimum(g, ntiles - 1) // tiles_per_seq, 0, 0, 0)),
        ],
        out_shape=[
            jax.ShapeDtypeStruct((n, d), x.dtype),
            jax.ShapeDtypeStruct((bsz, N_REC_HEADS, REC_DK, REC_DK), F32),
        ],
        scratch_shapes=[
            pltpu.VMEM((tm, ATTN_WIDTH + w), BF16),
            pltpu.VMEM((tm, d), F32),
            pltpu.VMEM((tm, d), BF16),
            pltpu.VMEM((tm, wg.shape[1]), BF16),
            pltpu.VMEM((tm, d), F32),
            pltpu.VMEM((N_REC_HEADS, REC_DK, REC_DK), F32),
            pltpu.VMEM((chunk, w), F32),
            pltpu.VMEM((chunk, w), F32),
            pltpu.VMEM((chunk, w), F32),
        ],
        compiler_params=pltpu.CompilerParams(
            dimension_semantics=("arbitrary",), vmem_limit_bytes=VMEM_LIMIT),
        name="mixffn",
    )(sinks, q, k, v, k, v, rq, lf, rk, ri, rg, nrm, lvl, x, wo, wg, wu, wd, g1, g2, g3)


def _dec_attn_body(sink_ref, q_ref, kn_ref, vn_ref, ck_ref, cv_ref, a_ref, nk_ref, nv_ref,
                   qs_ref, os_ref, *, gb):
    group = N_Q_HEADS // N_KV_HEADS
    lane = lax.broadcasted_iota(jnp.int32, (gb, LANES), 1)
    low = lane < HEAD_DIM
    q = q_ref[...]
    for hq in range(N_Q_HEADS):
        z = q[:, (hq // 2) * LANES:(hq // 2 + 1) * LANES]
        z = jnp.where(low, z, 0.0) if hq % 2 == 0 else jnp.where(low, 0.0, z)
        if hq % 2 != hq // group:
            z = pltpu.roll(z, HEAD_DIM, 1)
        qs_ref[hq * gb:(hq + 1) * gb, :] = z

    hrow = lax.broadcasted_iota(jnp.int32, (N_Q_HEADS, 1), 0)
    sink = jnp.zeros((N_Q_HEADS, 1), F32)
    for hq in range(N_Q_HEADS):
        sink = jnp.where(hrow == hq, sink_ref[hq], sink)
    pad = jnp.zeros((8, LANES), BF16)
    w = ck_ref.shape[1]
    last = lax.broadcasted_iota(jnp.int32, (w, LANES), 0) == w - 1

    for j in range(gb):
        kk = jnp.where(last, kn_ref[j:j + 1, :], pltpu.roll(ck_ref[j], w - 1, 0))
        vv = jnp.where(last, vn_ref[j:j + 1, :], pltpu.roll(cv_ref[j], w - 1, 0))
        nk_ref[j] = kk
        nv_ref[j] = vv
        qp = qs_ref[pl.ds(j, N_Q_HEADS, stride=gb), :]
        qp = jnp.concatenate([qp.astype(BF16), pad], axis=0)
        s = _dot_nt(qp, kk.astype(BF16))[0:N_Q_HEADS, :]
        m = jnp.maximum(jnp.max(s, axis=-1, keepdims=True), sink)
        p = jnp.exp(s - m)
        den = jnp.sum(p, axis=-1, keepdims=True) + jnp.exp(sink - m)
        pb = jnp.concatenate([p.astype(BF16), pad], axis=0)
        o = _dot(pb, vv.astype(BF16))[0:N_Q_HEADS, :] / den
        os_ref[j * N_Q_HEADS:(j + 1) * N_Q_HEADS, :] = o

    for cidx in range(ATTN_WIDTH // LANES):
        parts = []
        for half in range(2):
            hq = 2 * cidx + half
            z = os_ref[pl.ds(hq, gb, stride=N_Q_HEADS), :]
            if half != hq // group:
                z = pltpu.roll(z, HEAD_DIM, 1)
            parts.append(z)
        a_ref[:, cidx * LANES:(cidx + 1) * LANES] = jnp.where(low, parts[0], parts[1]).astype(a_ref.dtype)


def _dec_attn(q, k_new, v_new, cache_k, cache_v, sinks, *, gb):
    nb = q.shape[0]
    w = cache_k.shape[1]
    row = lambda i: (i, 0)
    blk3 = lambda i: (i, 0, 0)
    return pl.pallas_call(
        functools.partial(_dec_attn_body, gb=gb),
        grid=(nb // gb,),
        in_specs=[
            pl.BlockSpec(memory_space=pltpu.SMEM),
            pl.BlockSpec((gb, ATTN_WIDTH), row),
            pl.BlockSpec((gb, KV_WIDTH), row),
            pl.BlockSpec((gb, KV_WIDTH), row),
            pl.BlockSpec((gb, w, KV_WIDTH), blk3),
            pl.BlockSpec((gb, w, KV_WIDTH), blk3),
        ],
        out_specs=[
            pl.BlockSpec((gb, ATTN_WIDTH), row),
            pl.BlockSpec((gb, w, KV_WIDTH), blk3),
            pl.BlockSpec((gb, w, KV_WIDTH), blk3),
        ],
        out_shape=[
            jax.ShapeDtypeStruct((nb, ATTN_WIDTH), BF16),
            jax.ShapeDtypeStruct(cache_k.shape, cache_k.dtype),
            jax.ShapeDtypeStruct(cache_v.shape, cache_v.dtype),
        ],
        scratch_shapes=[
            pltpu.VMEM((N_Q_HEADS * gb, LANES), F32),
            pltpu.VMEM((N_Q_HEADS * gb, LANES), F32),
        ],
        compiler_params=pltpu.CompilerParams(
            dimension_semantics=("arbitrary",), vmem_limit_bytes=VMEM_LIMIT),
        name="dec_attn",
    )(sinks, q, k_new, v_new, cache_k, cache_v)


def _dec_hgrn_body(q_ref, k_ref, i_ref, g_ref, nrm_ref, s_ref, o_ref, sn_ref, os_ref, *, gb):
    dk = REC_DK
    zpad = jnp.zeros((dk - gb, dk), F32)
    for h in range(N_REC_HEADS):
        hs = slice(h * dk, (h + 1) * dk)
        qt = jnp.concatenate([q_ref[:, hs], zpad], axis=0).T
        kt = jnp.concatenate([k_ref[:, hs], zpad], axis=0).T
        for j in range(gb):
            kcol = kt[:, j:j + 1]
            s_new = s_ref[j, h] * (1.0 - kcol) + kcol * i_ref[j:j + 1, hs]
            sn_ref[j, h] = s_new
            os_ref[j:j + 1, hs] = jnp.sum(s_new * qt[:, j:j + 1], axis=0, keepdims=True)
    for h in range(N_REC_HEADS):
        hs = slice(h * dk, (h + 1) * dk)
        oh = os_ref[:, hs]
        oh = oh * lax.rsqrt(jnp.mean(oh * oh, axis=-1, keepdims=True) + EPS) * nrm_ref[h:h + 1, :]
        o_ref[:, hs] = (oh * g_ref[:, hs]).astype(o_ref.dtype)


def _dec_hgrn(q, k, iv, g, nrm, state, *, gb):
    nb, w = q.shape
    row = lambda i: (i, 0)
    blk4 = lambda i: (i, 0, 0, 0)
    return pl.pallas_call(
        functools.partial(_dec_hgrn_body, gb=gb),
        grid=(nb // gb,),
        in_specs=[
            pl.BlockSpec((gb, w), row),
            pl.BlockSpec((gb, w), row),
            pl.BlockSpec((gb, w), row),
            pl.BlockSpec((gb, w), row),
            pl.BlockSpec(nrm.shape, lambda i: (0, 0)),
            pl.BlockSpec((gb,) + state.shape[1:], blk4),
        ],
        out_specs=[
            pl.BlockSpec((gb, w), row),
            pl.BlockSpec((gb,) + state.shape[1:], blk4),
        ],
        out_shape=[
            jax.ShapeDtypeStruct((nb, w), BF16),
            jax.ShapeDtypeStruct(state.shape, state.dtype),
        ],
        scratch_shapes=[pltpu.VMEM((gb, w), F32)],
        compiler_params=pltpu.CompilerParams(
            dimension_semantics=("arbitrary",), vmem_limit_bytes=VMEM_LIMIT),
        name="dec_hgrn",
    )(q, k, iv, g, nrm, state)


def _ffn_body(x_ref, a_ref, o_ref, wo_ref, wg_ref, wu_ref, wd_ref, g1_ref, g2_ref, g3_ref, y_ref):
    _ffn_rows(x_ref, a_ref, o_ref, wo_ref, wg_ref, wu_ref, wd_ref, g1_ref, g2_ref, g3_ref, y_ref,
              slice(0, x_ref.shape[0]))


def _ffn(x, a, o, wo, wg, wu, wd, g1, g2, g3, *, tm):
    n, d = x.shape
    row = lambda i: (i, 0)
    const = lambda i: (0, 0)
    resident = lambda arr: pl.BlockSpec(arr.shape, const, pipeline_mode=pl.Buffered(1))
    return pl.pallas_call(
        _ffn_body,
        grid=(n // tm,),
        in_specs=[
            pl.BlockSpec((tm, d), row),
            pl.BlockSpec((tm, a.shape[1]), row),
            pl.BlockSpec((tm, o.shape[1]), row),
            resident(wo), resident(wg), resident(wu), resident(wd),
            pl.BlockSpec((1, d), const), pl.BlockSpec((1, d), const), pl.BlockSpec((1, d), const),
        ],
        out_specs=pl.BlockSpec((tm, d), row),
        out_shape=jax.ShapeDtypeStruct((n, d), x.dtype),
        compiler_params=pltpu.CompilerParams(
            dimension_semantics=("arbitrary",), vmem_limit_bytes=VMEM_LIMIT),
        name="ffn",
    )(x, a, o, wo, wg, wu, wd, g1, g2, g3)


def _rope_tables(pos):
    half = ROT_DIM // 2
    inv_freq = jnp.exp(-jnp.log(jnp.asarray(ROPE_THETA, F32)) * jnp.arange(half, dtype=F32) * (2.0 / ROT_DIM))
    d = np.arange(LANES) % HEAD_DIM
    rot = d < ROT_DIM
    freq = jnp.where(rot, inv_freq[d % half], 0.0)
    sign = jnp.asarray(np.where(d < half, -1.0, np.where(rot, 1.0, 0.0)), F32)
    ang = pos[:, None] * freq[None, :]
    sin = jnp.sin(ang)
    return jnp.cos(ang), sin, sin * sign[None, :]


def _pick_tile(n, want):
    t = min(n, want)
    while n % t:
        t //= 2
    return t


def kernel(x_prompt, x_sample, cache_k_win, cache_v_win, state_hgrn, w_in, w_out, w_gate, w_up, w_down,
           norm_pre_mix, norm_post_mix, norm_pre_ffn, norm_post_ffn, attn_sinks, rec_lb, rec_out_norm):
    depth = w_in.shape[0]
    assert depth == 1, "single-layer step"
    bsz, seq, d = x_prompt.shape
    nb, dec_seq, _ = x_sample.shape
    assert dec_seq == 1 and seq % WINDOW == 0
    w_keep = cache_k_win.shape[2]
    assert w_keep == WINDOW

    w_in_b = w_in[0].astype(BF16)
    w_out_b = w_out[0].astype(BF16)
    w_gate_b = w_gate[0].astype(BF16)
    w_up_b = w_up[0].astype(BF16)
    w_down_b = w_down[0].astype(BF16)
    g_pre, g_post, g_pre_f, g_post_f = norm_pre_mix, norm_post_mix, norm_pre_ffn, norm_post_ffn
    sinks = attn_sinks[0].astype(F32)
    nrm = rec_out_norm[0].astype(F32)

    tm = _pick_tile(seq, 512)
    tm_in = _pick_tile(seq, 1024)
    row_tabs = _rope_tables(jnp.arange(tm_in, dtype=F32))
    base_tabs = [t[:, None, :] for t in _rope_tables(jnp.arange(0, seq, tm_in, dtype=F32))]
    xp = x_prompt.reshape(bsz * seq, d)
    q, k, v, rq, lf, rk, ri, rg = _inproj(xp, g_pre, w_in_b, row_tabs, base_tabs, rec_lb,
                                          tm=tm_in, act_dtype=BF16)
    r3 = lambda z: z.reshape(bsz, seq, z.shape[-1])
    k3, v3 = r3(k), r3(v)
    yp, s_fin = _mixffn(xp, q, k, v, rq, lf, rk, ri, rg, sinks, nrm, w_out_b, w_gate_b, w_up_b, w_down_b,
                        g_post, g_pre_f, g_post_f, bsz=bsz, tm=tm, chunk=128)
    y_prompt = yp.reshape(bsz, seq, d)
    keep = min(WINDOW, seq)
    new_k_p = k3[:, seq - keep:].reshape(1, bsz, keep, N_KV_HEADS, HEAD_DIM).astype(cache_k_win.dtype)
    new_v_p = v3[:, seq - keep:].reshape(1, bsz, keep, N_KV_HEADS, HEAD_DIM).astype(cache_v_win.dtype)
    new_s_p = s_fin[None].astype(state_hgrn.dtype)

    row_tabs = _rope_tables(jnp.full((nb,), float(PAST_LEN), F32))
    base_tabs = [t[:, None, :] for t in _rope_tables(jnp.zeros((1,), F32))]
    xs = x_sample.reshape(nb, d)
    qs, ks, vs, rqs, _, rks, ris, rgs = _inproj(xs, g_pre, w_in_b, row_tabs, base_tabs, rec_lb,
                                                tm=nb, act_dtype=F32)
    gb = 8
    ck = cache_k_win[0].reshape(nb, w_keep, KV_WIDTH)
    cv = cache_v_win[0].reshape(nb, w_keep, KV_WIDTH)
    a_s, nk, nv = _dec_attn(qs, ks, vs, ck, cv, sinks, gb=gb)
    o_s, s_new = _dec_hgrn(rqs, rks, ris, rgs, nrm, state_hgrn[0], gb=gb)
    ys = _ffn(xs, a_s, o_s, w_out_b, w_gate_b, w_up_b, w_down_b, g_post, g_pre_f, g_post_f, tm=nb)
    y_sample = ys.reshape(nb, 1, d)
    new_k_s = nk.reshape(1, nb, w_keep, N_KV_HEADS, HEAD_DIM)
    new_v_s = nv.reshape(1, nb, w_keep, N_KV_HEADS, HEAD_DIM)
    new_s_s = s_new[None]

    return (y_prompt, y_sample, new_k_p, new_v_p, new_s_p, new_k_s, new_v_s, new_s_s)
```

```python
import functools

import jax
import jax.numpy as jnp
import numpy as np
from jax import lax
from jax.experimental import pallas as pl
from jax.experimental.pallas import tpu as pltpu

F32 = jnp.float32
BF16 = jnp.bfloat16

PAST_LEN = 16384
WINDOW = 128
HEAD_DIM = 64
N_Q_HEADS = 8
N_KV_HEADS = 2
ROT_DIM = HEAD_DIM // 4
ROPE_THETA = 500000.0
N_REC_HEADS = 4
REC_DK = 128
EPS = 1e-6

ATTN_WIDTH = N_Q_HEADS * HEAD_DIM
KV_WIDTH = N_KV_HEADS * HEAD_DIM
REC_WIDTH = N_REC_HEADS * REC_DK
LANES = 128
NEG = -1e30

VMEM_LIMIT = 56 * 1024 * 1024


def _dot(a, b):
    return jnp.dot(a, b, preferred_element_type=F32)


def _dot_nt(a, b):
    return lax.dot_general(a, b, (((1,), (1,)), ((), ())), preferred_element_type=F32)


def _dot_tn(a, b):
    return lax.dot_general(a, b, (((0,), (0,)), ((), ())), preferred_element_type=F32)


def _sigmoid(x):
    return 1.0 / (1.0 + jnp.exp(-x))


def _rms(x, g):
    return x * lax.rsqrt(jnp.mean(x * x, axis=-1, keepdims=True) + EPS) * g


def _inproj_body(x_ref, g_ref, w_ref, cr_ref, sr_ref, srs_ref, cb_ref, sb_ref, sbs_ref, lbp_ref,
                 q_ref, k_ref, v_ref, rq_ref, lf_ref, rk_ref, ri_ref, rg_ref):
    h = _rms(x_ref[...], g_ref[...]).astype(BF16)
    cr = cr_ref[...]
    cos = cr * cb_ref[...] - sr_ref[...] * sb_ref[...]
    sin = srs_ref[...] * cb_ref[...] + cr * sbs_ref[...]
    lane = lax.broadcasted_iota(jnp.int32, cos.shape, 1)
    first = (lane & (HEAD_DIM - 1)) < (ROT_DIM // 2)

    def rope(z):
        partner = jnp.where(first, pltpu.roll(z, LANES - ROT_DIM // 2, 1), pltpu.roll(z, ROT_DIM // 2, 1))
        return z * cos + partner * sin

    def proj(lo, width):
        return _dot(h, w_ref[:, lo:lo + width])

    scale = HEAD_DIM ** -0.5
    zq = proj(0, ATTN_WIDTH)
    for j in range(ATTN_WIDTH // LANES):
        z = rope(zq[:, j * LANES:(j + 1) * LANES])
        q_ref[:, j * LANES:(j + 1) * LANES] = (z * scale).astype(q_ref.dtype)
    off = ATTN_WIDTH
    zkv = proj(off, 2 * KV_WIDTH)
    k_ref[...] = rope(zkv[:, :KV_WIDTH])
    v_ref[...] = zkv[:, KV_WIDTH:]
    off += 2 * KV_WIDTH

    z = proj(off, REC_WIDTH)
    rq_ref[...] = (z * _sigmoid(z)).astype(rq_ref.dtype)
    off += REC_WIDTH

    r = lbp_ref[...]
    e = jnp.exp(r - jnp.max(r, axis=0, keepdims=True))
    lb = e[0:1, :] / jnp.sum(e, axis=0, keepdims=True)
    z = proj(off, REC_WIDTH)
    f = lb + (1.0 - lb) * _sigmoid(z)
    lf_ref[...] = jnp.log(f)
    rk_ref[...] = (1.0 - f).astype(rk_ref.dtype)
    off += REC_WIDTH

    ri_ref[...] = proj(off, REC_WIDTH).astype(ri_ref.dtype)
    off += REC_WIDTH
    z = proj(off, REC_WIDTH)
    rg_ref[...] = (z * _sigmoid(z)).astype(rg_ref.dtype)


def _inproj(x, g, w_bf, row_tabs, base_tabs, rec_lb, *, tm, act_dtype):
    n, d = x.shape
    n_base = base_tabs[0].shape[0]
    in_w = w_bf.shape[1]
    row = lambda i: (i, 0)
    const = lambda i: (0, 0)
    base = lambda i: (i % n_base, 0, 0)
    outs = [
        jax.ShapeDtypeStruct((n, ATTN_WIDTH), act_dtype),
        jax.ShapeDtypeStruct((n, KV_WIDTH), F32),
        jax.ShapeDtypeStruct((n, KV_WIDTH), F32),
        jax.ShapeDtypeStruct((n, REC_WIDTH), act_dtype),
        jax.ShapeDtypeStruct((n, REC_WIDTH), F32),
        jax.ShapeDtypeStruct((n, REC_WIDTH), act_dtype),
        jax.ShapeDtypeStruct((n, REC_WIDTH), act_dtype),
        jax.ShapeDtypeStruct((n, REC_WIDTH), act_dtype),
    ]
    return pl.pallas_call(
        _inproj_body,
        grid=(n // tm,),
        in_specs=[
            pl.BlockSpec((tm, d), row),
            pl.BlockSpec((1, d), const),
            pl.BlockSpec((d, in_w), const, pipeline_mode=pl.Buffered(1)),
            pl.BlockSpec((tm, LANES), const), pl.BlockSpec((tm, LANES), const), pl.BlockSpec((tm, LANES), const),
            pl.BlockSpec((None, 1, LANES), base), pl.BlockSpec((None, 1, LANES), base),
            pl.BlockSpec((None, 1, LANES), base),
            pl.BlockSpec(rec_lb.shape, const),
        ],
        out_specs=[pl.BlockSpec((tm, o.shape[1]), row) for o in outs],
        out_shape=outs,
        compiler_params=pltpu.CompilerParams(
            dimension_semantics=("arbitrary",), vmem_limit_bytes=VMEM_LIMIT),
        name="inproj",
    )(x, g, w_bf, *row_tabs, *base_tabs, rec_lb)


def _head_variants(z):
    lane = lax.broadcasted_iota(jnp.int32, z.shape, 1)
    low = lane < HEAD_DIM
    z0 = jnp.where(low, z, 0.0)
    z1 = jnp.where(low, 0.0, z)
    return {
        (0, 0): z0.astype(BF16),
        (1, 1): z1.astype(BF16),
        (0, 1): pltpu.roll(z0, HEAD_DIM, 1).astype(BF16),
        (1, 0): pltpu.roll(z1, HEAD_DIM, 1).astype(BF16),
    }


def _swa_steps(nblk):
    return (nblk + 1) + nblk * N_KV_HEADS * (1 + (N_Q_HEADS // N_KV_HEADS // 2) * 3)


def _swa_tile(sink_ref, q_ref, kc_ref, vc_ref, kp_ref, vp_ref, a_ref, has_prev, nblk):
    w = WINDOW
    key = lax.broadcasted_iota(jnp.int32, (w, w), 0)
    qry = lax.broadcasted_iota(jnp.int32, (w, w), 1)
    cur_ok = key <= qry
    prev_band = key > qry

    kvar, vt = [], []
    for j in range(-1, nblk):
        kblk = kp_ref[...] if j < 0 else kc_ref[j * w:(j + 1) * w, :]
        vblk = vp_ref[...] if j < 0 else vc_ref[j * w:(j + 1) * w, :]
        kvar.append(_head_variants(kblk))
        vt.append(vblk.T.astype(BF16))
        yield

    group = N_Q_HEADS // N_KV_HEADS
    for j in range(nblk):
        prev_ok = prev_band if j > 0 else jnp.logical_and(prev_band, has_prev)
        for h in range(N_KV_HEADS):
            lhs = jnp.concatenate([kvar[j][(h, 0)], kvar[j + 1][(h, 0)],
                                   kvar[j][(h, 1)], kvar[j + 1][(h, 1)]], axis=0)
            vth = jnp.concatenate([vt[j][h * HEAD_DIM:(h + 1) * HEAD_DIM, :],
                                   vt[j + 1][h * HEAD_DIM:(h + 1) * HEAD_DIM, :]], axis=1)
            c0 = h * (group // 2)
            q2 = jnp.concatenate([q_ref[j * w:(j + 1) * w, (c0 + cc) * LANES:(c0 + cc + 1) * LANES]
                                  for cc in range(group // 2)], axis=0)
            st = _dot_nt(lhs, q2)
            yield
            for cc in range(group // 2):
                c = c0 + cc
                qcols = slice(cc * w, (cc + 1) * w)
                ps, rinv = [], []
                for half in range(2):
                    sink = sink_ref[2 * c + half]
                    base = half * 2 * w
                    sp = jnp.where(prev_ok, st[base:base + w, qcols], NEG)
                    sc = jnp.where(cur_ok, st[base + w:base + 2 * w, qcols], NEG)
                    m = jnp.maximum(jnp.max(sp, axis=0, keepdims=True), jnp.max(sc, axis=0, keepdims=True))
                    m = jnp.maximum(m, sink)
                    pp = jnp.exp(sp - m)
                    pc = jnp.exp(sc - m)
                    den = (jnp.sum(pp, axis=0, keepdims=True) + jnp.sum(pc, axis=0, keepdims=True)
                           + jnp.exp(sink - m))
                    ps.append(jnp.concatenate([pp, pc], axis=0).astype(BF16))
                    rinv.append(1.0 / den)
                    yield
                ot2 = _dot(vth, jnp.concatenate(ps, axis=1))
                ot = jnp.concatenate([ot2[:, :w] * rinv[0], ot2[:, w:] * rinv[1]], axis=0)
                a_ref[j * w:(j + 1) * w, c * LANES:(c + 1) * LANES] = ot.T.astype(a_ref.dtype)
                yield


def _split3(x):
    hi = x.astype(BF16)
    r1 = x - hi.astype(F32)
    mid = r1.astype(BF16)
    lo = (r1 - mid.astype(F32)).astype(BF16)
    return hi, mid, lo


def _level_table(c):
    t = np.arange(c)[:, None]
    s = np.arange(c)[None, :]
    x = t ^ s
    lvl = np.where(x > 0, np.floor(np.log2(np.maximum(x, 1))).astype(np.int32), 0)
    nlev = int(np.log2(c))
    return np.where(t > s, lvl, np.where(t == s, nlev, -1)).astype(np.int32)


HEAD_PAIR = 2 * REC_DK


def _hgrn_steps(chunk, nchunk):
    return nchunk * (1 + (N_REC_HEADS // 2) * (chunk // 32 + 3))


def _block_diag(a):
    z = jnp.zeros_like(a[:, :REC_DK])
    return jnp.concatenate([jnp.concatenate([a[:, :REC_DK], z], axis=1),
                            jnp.concatenate([z, a[:, REC_DK:]], axis=1)], axis=0)


def _hgrn_tile(q_ref, lf_ref, k_ref, v_ref, g_ref, nrm_ref, lvl_ref, o_ref,
               st_ref, b_ref, q32_ref, k32_ref, chunk, nchunk):
    c = chunk
    sub = 8
    blk = 16
    nlev = c.bit_length() - 1
    first_slab = 4
    row = lax.broadcasted_iota(jnp.int32, (c, c), 0)
    col = lax.broadcasted_iota(jnp.int32, (c, c), 1)
    tri = (col <= row).astype(BF16)
    tri2 = jnp.concatenate([tri, tri], axis=1)
    brow_i = lax.broadcasted_iota(jnp.int32, (blk, HEAD_PAIR), 0)
    odd = (brow_i & 1) != 0
    side1 = (brow_i & 2) != 0
    side2 = (brow_i & 4) != 0
    side3 = (brow_i & 8) != 0
    low4 = lax.broadcasted_iota(jnp.int32, (sub, HEAD_PAIR), 0) < 4
    lane = lax.broadcasted_iota(jnp.int32, (sub, c), 1)
    cat = lambda parts: parts[0] if len(parts) == 1 else jnp.concatenate(parts, axis=0)

    for ci in range(nchunk):
        sl = slice(ci * c, (ci + 1) * c)
        hi, mid, lo = _split3(lf_ref[sl, :])
        b_ref[...] = _dot(tri2, jnp.concatenate([hi, mid], axis=0)) + _dot(tri, lo)
        q32_ref[...] = q_ref[sl, :].astype(F32)
        k32_ref[...] = k_ref[sl, :].astype(F32)
        lvl = lvl_ref[...]
        yield

        for hp in range(N_REC_HEADS // 2):
            ps = slice(hp * HEAD_PAIR, (hp + 1) * HEAD_PAIR)

            def bcast(r, n=sub):
                return jnp.broadcast_to(b_ref[r:r + 1, ps], (n, HEAD_PAIR))

            blast = b_ref[c - 1:c, ps]
            q_in, k_out = [], []
            lev = [[] for _ in range(nlev)]
            lev_q = [[] for _ in range(nlev)]
            for r0 in range(0, c, blk):
                q = q32_ref[r0:r0 + blk, ps]
                k = k32_ref[r0:r0 + blk, ps]
                b = b_ref[r0:r0 + blk, ps]
                lf = lf_ref[ci * c + r0:ci * c + r0 + blk, ps]
                q_in.append((q * jnp.exp(b)).astype(BF16))
                k_out.append((k * jnp.exp(blast - b)).astype(BF16))
                lev[0].append((jnp.where(odd, q, k) * jnp.exp(jnp.where(odd, lf, 0.0))).astype(BF16))
                bref = jnp.concatenate([jnp.where(low4, bcast(r0 + 1), bcast(r0 + 5)),
                                        jnp.where(low4, bcast(r0 + 9), bcast(r0 + 13))], axis=0)
                lev[1].append((jnp.where(side1, q, k) * jnp.exp(-jnp.abs(b - bref))).astype(BF16))
                bref = jnp.concatenate([bcast(r0 + 3), bcast(r0 + 11)], axis=0)
                lev[2].append((jnp.where(side2, q, k) * jnp.exp(-jnp.abs(b - bref))).astype(BF16))
                bref = bcast(r0 + 7, blk)
                lev[3].append((jnp.where(side3, q, k) * jnp.exp(-jnp.abs(b - bref))).astype(BF16))
                for l in range(first_slab, nlev):
                    m = 1 << l
                    g0 = (r0 // (2 * m)) * (2 * m)
                    bref = bcast(g0 + m - 1, blk)
                    if r0 - g0 < m:
                        lev[l].append((k * jnp.exp(bref - b)).astype(BF16))
                    else:
                        piece = (q * jnp.exp(b - bref)).astype(BF16)
                        lev[l].append(piece)
                        lev_q[l].append(piece)
                if r0 % (2 * blk) == blk:
                    yield
            q_in, k_out = cat(q_in), cat(k_out)

            halves = [slice(0, REC_DK), slice(REC_DK, HEAD_PAIR)]
            pd = _dot_nt(q_ref[sl, ps], _block_diag(k_ref[sl, ps]))
            sc = [jnp.where(lvl == nlev, pd[:, hv], 0.0) for hv in halves]
            for l in range(first_slab):
                a = cat(lev[l])
                p = _dot_nt(a, _block_diag(a))
                sc = [jnp.where(lvl == l, p[:, hv], s) for hv, s in zip(halves, sc)]
            yield
            blocks = [[s[r0:r0 + sub, :] for r0 in range(0, c, sub)] for s in sc]
            for l in range(first_slab, nlev):
                m = 1 << l
                pq = _dot_nt(cat(lev_q[l]), _block_diag(cat(lev[l])))
                for gi, g0 in enumerate(range(0, c, 2 * m)):
                    in_group = jnp.logical_and(lane >= g0, lane < g0 + m)
                    for r in range(0, m, sub):
                        bi = (g0 + m + r) // sub
                        for hh, hv in enumerate(halves):
                            blocks[hh][bi] = jnp.where(in_group, pq[gi * m + r:gi * m + r + sub, hv], blocks[hh][bi])
            sc2 = jnp.concatenate([jnp.concatenate(bl, axis=0) for bl in blocks], axis=1).astype(BF16)
            yield
            st2 = jnp.concatenate([st_ref[2 * hp], st_ref[2 * hp + 1]], axis=1)
            oh2 = (_dot_nt(q_in, _block_diag(st2.astype(BF16)))
                   + _dot(sc2, _block_diag(v_ref[sl, ps])))
            for hh, hv in enumerate(halves):
                h = 2 * hp + hh
                hs = slice(h * REC_DK, (h + 1) * REC_DK)
                st_ref[h] = st2[:, hv] * jnp.exp(b_ref[c - 1:c, hs]) + _dot_tn(v_ref[sl, hs], k_out[:, hv])
                oh = oh2[:, hv]
                oh = oh * lax.rsqrt(jnp.mean(oh * oh, axis=-1, keepdims=True) + EPS) * nrm_ref[h:h + 1, :]
                o_ref[sl, hs] = (oh * g_ref[sl, hs].astype(F32)).astype(o_ref.dtype)
            yield


def _ffn_rows(x_ref, a_ref, o_ref, wo_ref, wg_ref, wu_ref, wd_ref, g1_ref, g2_ref, g3_ref, y_ref, rows):
    aw = a_ref.shape[1]
    mix = _dot(a_ref[rows, :], wo_ref[0:aw, :]) + _dot(o_ref[rows, :], wo_ref[aw:, :])
    x1 = x_ref[rows, :] + _rms(mix, g1_ref[...])
    h2 = _rms(x1, g2_ref[...]).astype(BF16)
    gate = _dot(h2, wg_ref[...])
    up = _dot(h2, wu_ref[...])
    act = (gate * _sigmoid(gate) * up).astype(BF16)
    ffn = _dot(act, wd_ref[...])
    y_ref[rows, :] = x1 + _rms(ffn, g3_ref[...])


FFN_COLS = 256
FFN_ROWS = 256
NORM_ROWS = 128


def _post_steps(tm):
    return (tm // FFN_ROWS) * (1 + FFN_ROWS // NORM_ROWS)


def _post_tile(x_ref, mix_ref, wo_ref, g1_ref, g2_ref, x1_ref, h2_ref):
    tm = x_ref.shape[0]
    for r0 in range(0, tm, FFN_ROWS):
        mix = _dot(mix_ref[r0:r0 + FFN_ROWS, :], wo_ref[...])
        yield
        for r1 in range(0, FFN_ROWS, NORM_ROWS):
            rows = slice(r0 + r1, r0 + r1 + NORM_ROWS)
            x1 = x_ref[rows, :] + _rms(mix[r1:r1 + NORM_ROWS, :], g1_ref[...])
            x1_ref[rows, :] = x1
            h2_ref[rows, :] = _rms(x1, g2_ref[...]).astype(BF16)
            yield


def _ffn_steps(tm, d, d_ff):
    return (tm // FFN_ROWS) * (d_ff // FFN_COLS + d // FFN_COLS) + tm // NORM_ROWS


def _ffn_tile(wg_ref, wu_ref, wd_ref, g3_ref, y_ref, x1_ref, h2_ref, act_ref, acc_ref):
    tm, d = x1_ref.shape
    d_ff = wg_ref.shape[1]
    for c0 in range(0, d_ff, FFN_COLS):
        cols = slice(c0, c0 + FFN_COLS)
        for r0 in range(0, tm, FFN_ROWS):
            rows = slice(r0, r0 + FFN_ROWS)
            gate = _dot(h2_ref[rows, :], wg_ref[:, cols])
            up = _dot(h2_ref[rows, :], wu_ref[:, cols])
            act_ref[rows, cols] = (gate * _sigmoid(gate) * up).astype(BF16)
            yield
    for c0 in range(0, d, FFN_COLS):
        cols = slice(c0, c0 + FFN_COLS)
        for r0 in range(0, tm, FFN_ROWS):
            rows = slice(r0, r0 + FFN_ROWS)
            acc_ref[rows, cols] = _dot(act_ref[rows, :], wd_ref[:, cols])
            yield
    for r0 in range(0, tm, NORM_ROWS):
        rows = slice(r0, r0 + NORM_ROWS)
        y_ref[rows, :] = x1_ref[rows, :] + _rms(acc_ref[rows, :], g3_ref[...])
        yield


def _merged(*streams):
    live = [[0, n, gen] for gen, n in streams]
    while live:
        cur = min(live, key=lambda e: e[0] / e[1])
        try:
            next(cur[2])
            cur[0] += 1
            yield
        except StopIteration:
            live.remove(cur)


def _chain(*gens):
    for gen in gens:
        yield from gen


def _mixffn_body(sink_ref, q_ref, kc_ref, vc_ref, kp_ref, vp_ref, rq_ref, lf_ref, rk_ref, ri_ref, rg_ref,
                 nrm_ref, lvl_ref, x_ref, wo_ref, wg_ref, wu_ref, wd_ref, g1_ref, g2_ref, g3_ref,
                 y_ref, sfin_ref, mix_scr, x1_scr, h2_scr, act_scr, acc_scr,
                 st_ref, b_ref, q32_ref, k32_ref, *, tiles_per_seq, ntiles, nblk, chunk, nchunk):
    g = pl.program_id(0)
    real = g < ntiles
    t_in = jnp.minimum(g, ntiles - 1) % tiles_per_seq
    cur = g % 2

    @pl.when(g == 0)
    def _():
        x1_scr[0] = jnp.zeros(x1_scr.shape[1:], x1_scr.dtype)
        h2_scr[0] = jnp.zeros(h2_scr.shape[1:], h2_scr.dtype)

    @pl.when(jnp.logical_and(real, t_in == 0))
    def _():
        st_ref[...] = jnp.zeros_like(st_ref)

    tm, d = x_ref.shape
    n_mix = _swa_steps(nblk) + _hgrn_steps(chunk, nchunk)
    mixers = _merged(
        (_swa_tile(sink_ref, q_ref, kc_ref, vc_ref, kp_ref, vp_ref, mix_scr.at[:, 0:ATTN_WIDTH], t_in > 0, nblk),
         _swa_steps(nblk)),
        (_hgrn_tile(rq_ref, lf_ref, rk_ref, ri_ref, rg_ref, nrm_ref, lvl_ref, mix_scr.at[:, ATTN_WIDTH:],
                    st_ref, b_ref, q32_ref, k32_ref, chunk, nchunk), _hgrn_steps(chunk, nchunk)))
    post = _post_tile(x_ref, mix_scr, wo_ref, g1_ref, g2_ref, x1_scr.at[1 - cur], h2_scr.at[1 - cur])
    for _ in _merged(
            (_ffn_tile(wg_ref, wu_ref, wd_ref, g3_ref, y_ref, x1_scr.at[cur], h2_scr.at[cur], act_scr, acc_scr),
             _ffn_steps(tm, d, wg_ref.shape[1])),
            (_chain(mixers, post), n_mix + _post_steps(tm))):
        pass

    @pl.when(jnp.logical_and(real, t_in == tiles_per_seq - 1))
    def _():
        for h in range(N_REC_HEADS):
            sfin_ref[h] = st_ref[h].T


def _mixffn(x, q, k, v, rq, lf, rk, ri, rg, sinks, nrm, wo, wg, wu, wd, g1, g2, g3, *, bsz, tm, chunk):
    n, d = x.shape
    ntiles = n // tm
    tiles_per_seq = ntiles // bsz
    nblk = tm // WINDOW
    w = rq.shape[1]
    lvl = jnp.asarray(_level_table(chunk))
    mix = lambda g: (jnp.minimum(g, ntiles - 1), 0)
    prev = lambda g: (jnp.maximum(jnp.minimum(g, ntiles - 1) * nblk - 1, 0), 0)
    ffn = lambda g: (jnp.maximum(g - 1, 0), 0)
    const = lambda g: (0, 0)
    resident = lambda arr: pl.BlockSpec(arr.shape, const, pipeline_mode=pl.Buffered(1))
    return pl.pallas_call(
        functools.partial(_mixffn_body, tiles_per_seq=tiles_per_seq, ntiles=ntiles, nblk=nblk,
                          chunk=chunk, nchunk=tm // chunk),
        grid=(ntiles + 1,),
        in_specs=[
            pl.BlockSpec(memory_space=pltpu.SMEM),
            pl.BlockSpec((tm, ATTN_WIDTH), mix),
            pl.BlockSpec((tm, KV_WIDTH), mix),
            pl.BlockSpec((tm, KV_WIDTH), mix),
            pl.BlockSpec((WINDOW, KV_WIDTH), prev),
            pl.BlockSpec((WINDOW, KV_WIDTH), prev),
            pl.BlockSpec((tm, w), mix),
            pl.BlockSpec((tm, w), mix),
            pl.BlockSpec((tm, w), mix),
            pl.BlockSpec((tm, w), mix),
            pl.BlockSpec((tm, w), mix),
            pl.BlockSpec(nrm.shape, const),
            pl.BlockSpec((chunk, chunk), const),
            pl.BlockSpec((tm, d), mix),
            resident(wo), resident(wg), resident(wu), resident(wd),
            pl.BlockSpec((1, d), const), pl.BlockSpec((1, d), const), pl.BlockSpec((1, d), const),
        ],
        out_specs=[
            pl.BlockSpec((tm, d), ffn),
            pl.BlockSpec((None, N_REC_HEADS, REC_DK, REC_DK),
                         lambda g: (jnp.minimum(g, ntiles - 1) // tiles_per_seq, 0, 0, 0)),
        ],
        out_shape=[
            jax.ShapeDtypeStruct((n, d), x.dtype),
            jax.ShapeDtypeStruct((bsz, N_REC_HEADS, REC_DK, REC_DK), F32),
        ],
        scratch_shapes=[
            pltpu.VMEM((tm, ATTN_WIDTH + w), BF16),
            pltpu.VMEM((2, tm, d), F32),
            pltpu.VMEM((2, tm, d), BF16),
            pltpu.VMEM((tm, wg.shape[1]), BF16),
            pltpu.VMEM((tm, d), F32),
            pltpu.VMEM((N_REC_HEADS, REC_DK, REC_DK), F32),
            pltpu.VMEM((chunk, w), F32),
            pltpu.VMEM((chunk, w), F32),
            pltpu.VMEM((chunk, w), F32),
        ],
        compiler_params=pltpu.CompilerParams(
            dimension_semantics=("arbitrary",), vmem_limit_bytes=VMEM_LIMIT),
        name="mixffn",
    )(sinks, q, k, v, k, v, rq, lf, rk, ri, rg, nrm, lvl, x, wo, wg, wu, wd, g1, g2, g3)


def _dec_attn_body(sink_ref, q_ref, kn_ref, vn_ref, ck_ref, cv_ref, a_ref, nk_ref, nv_ref,
                   qs_ref, os_ref, *, gb):
    group = N_Q_HEADS // N_KV_HEADS
    lane = lax.broadcasted_iota(jnp.int32, (gb, LANES), 1)
    low = lane < HEAD_DIM
    q = q_ref[...]
    for hq in range(N_Q_HEADS):
        z = q[:, (hq // 2) * LANES:(hq // 2 + 1) * LANES]
        z = jnp.where(low, z, 0.0) if hq % 2 == 0 else jnp.where(low, 0.0, z)
        if hq % 2 != hq // group:
            z = pltpu.roll(z, HEAD_DIM, 1)
        qs_ref[hq * gb:(hq + 1) * gb, :] = z

    hrow = lax.broadcasted_iota(jnp.int32, (N_Q_HEADS, 1), 0)
    sink = jnp.zeros((N_Q_HEADS, 1), F32)
    for hq in range(N_Q_HEADS):
        sink = jnp.where(hrow == hq, sink_ref[hq], sink)
    pad = jnp.zeros((8, LANES), BF16)
    w = ck_ref.shape[1]
    last = lax.broadcasted_iota(jnp.int32, (w, LANES), 0) == w - 1

    scores = []
    for j in range(gb):
        kk = jnp.where(last, kn_ref[j:j + 1, :], pltpu.roll(ck_ref[j], w - 1, 0))
        nk_ref[j] = kk
        qp = qs_ref[pl.ds(j, N_Q_HEADS, stride=gb), :]
        qp = jnp.concatenate([qp.astype(BF16), pad], axis=0)
        scores.append(_dot_nt(qp, kk.astype(BF16))[0:N_Q_HEADS, :])
    probs, dens = [], []
    for s in scores:
        m = jnp.maximum(jnp.max(s, axis=-1, keepdims=True), sink)
        p = jnp.exp(s - m)
        dens.append(jnp.sum(p, axis=-1, keepdims=True) + jnp.exp(sink - m))
        probs.append(jnp.concatenate([p.astype(BF16), pad], axis=0))
    for j in range(gb):
        vv = jnp.where(last, vn_ref[j:j + 1, :], pltpu.roll(cv_ref[j], w - 1, 0))
        nv_ref[j] = vv
        o = _dot(probs[j], vv.astype(BF16))[0:N_Q_HEADS, :] / dens[j]
        os_ref[j * N_Q_HEADS:(j + 1) * N_Q_HEADS, :] = o

    for cidx in range(ATTN_WIDTH // LANES):
        parts = []
        for half in range(2):
            hq = 2 * cidx + half
            z = os_ref[pl.ds(hq, gb, stride=N_Q_HEADS), :]
            if half != hq // group:
                z = pltpu.roll(z, HEAD_DIM, 1)
            parts.append(z)
        a_ref[:, cidx * LANES:(cidx + 1) * LANES] = jnp.where(low, parts[0], parts[1]).astype(a_ref.dtype)


def _dec_attn(q, k_new, v_new, cache_k, cache_v, sinks, *, gb):
    nb = q.shape[0]
    w = cache_k.shape[1]
    row = lambda i: (i, 0)
    blk3 = lambda i: (i, 0, 0)
    return pl.pallas_call(
        functools.partial(_dec_attn_body, gb=gb),
        grid=(nb // gb,),
        in_specs=[
            pl.BlockSpec(memory_space=pltpu.SMEM),
            pl.BlockSpec((gb, ATTN_WIDTH), row),
            pl.BlockSpec((gb, KV_WIDTH), row),
            pl.BlockSpec((gb, KV_WIDTH), row),
            pl.BlockSpec((gb, w, KV_WIDTH), blk3),
            pl.BlockSpec((gb, w, KV_WIDTH), blk3),
        ],
        out_specs=[
            pl.BlockSpec((gb, ATTN_WIDTH), row),
            pl.BlockSpec((gb, w, KV_WIDTH), blk3),
            pl.BlockSpec((gb, w, KV_WIDTH), blk3),
        ],
        out_shape=[
            jax.ShapeDtypeStruct((nb, ATTN_WIDTH), BF16),
            jax.ShapeDtypeStruct(cache_k.shape, cache_k.dtype),
            jax.ShapeDtypeStruct(cache_v.shape, cache_v.dtype),
        ],
        scratch_shapes=[
            pltpu.VMEM((N_Q_HEADS * gb, LANES), F32),
            pltpu.VMEM((N_Q_HEADS * gb, LANES), F32),
        ],
        compiler_params=pltpu.CompilerParams(
            dimension_semantics=("arbitrary",), vmem_limit_bytes=VMEM_LIMIT),
        name="dec_attn",
    )(sinks, q, k_new, v_new, cache_k, cache_v)


def _dec_hgrn_body(q_ref, k_ref, i_ref, g_ref, nrm_ref, s_ref, o_ref, sn_ref, os_ref, *, gb):
    dk = REC_DK
    zpad = jnp.zeros((dk - gb, dk), F32)
    for h in range(N_REC_HEADS):
        hs = slice(h * dk, (h + 1) * dk)
        kt = jnp.concatenate([k_ref[:, hs], zpad], axis=0).T
        for j in range(gb):
            kcol = kt[:, j:j + 1]
            sn_ref[j, h] = s_ref[j, h] * (1.0 - kcol) + kcol * i_ref[j:j + 1, hs]
        for j in range(gb):
            qrow = jnp.broadcast_to(q_ref[j:j + 1, hs], (16, dk)).astype(BF16)
            os_ref[j:j + 1, hs] = _dot(qrow, sn_ref[j, h].astype(BF16))[0:1, :]
    for h in range(N_REC_HEADS):
        hs = slice(h * dk, (h + 1) * dk)
        oh = os_ref[:, hs]
        oh = oh * lax.rsqrt(jnp.mean(oh * oh, axis=-1, keepdims=True) + EPS) * nrm_ref[h:h + 1, :]
        o_ref[:, hs] = (oh * g_ref[:, hs]).astype(o_ref.dtype)


def _dec_hgrn(q, k, iv, g, nrm, state, *, gb):
    nb, w = q.shape
    row = lambda i: (i, 0)
    blk4 = lambda i: (i, 0, 0, 0)
    return pl.pallas_call(
        functools.partial(_dec_hgrn_body, gb=gb),
        grid=(nb // gb,),
        in_specs=[
            pl.BlockSpec((gb, w), row),
            pl.BlockSpec((gb, w), row),
            pl.BlockSpec((gb, w), row),
            pl.BlockSpec((gb, w), row),
            pl.BlockSpec(nrm.shape, lambda i: (0, 0)),
            pl.BlockSpec((gb,) + state.shape[1:], blk4),
        ],
        out_specs=[
            pl.BlockSpec((gb, w), row),
            pl.BlockSpec((gb,) + state.shape[1:], blk4),
        ],
        out_shape=[
            jax.ShapeDtypeStruct((nb, w), BF16),
            jax.ShapeDtypeStruct(state.shape, state.dtype),
        ],
        scratch_shapes=[pltpu.VMEM((gb, w), F32)],
        compiler_params=pltpu.CompilerParams(
            dimension_semantics=("arbitrary",), vmem_limit_bytes=VMEM_LIMIT),
        name="dec_hgrn",
    )(q, k, iv, g, nrm, state)


def _ffn_body(x_ref, a_ref, o_ref, wo_ref, wg_ref, wu_ref, wd_ref, g1_ref, g2_ref, g3_ref, y_ref):
    _ffn_rows(x_ref, a_ref, o_ref, wo_ref, wg_ref, wu_ref, wd_ref, g1_ref, g2_ref, g3_ref, y_ref,
              slice(0, x_ref.shape[0]))


def _ffn(x, a, o, wo, wg, wu, wd, g1, g2, g3, *, tm):
    n, d = x.shape
    row = lambda i: (i, 0)
    const = lambda i: (0, 0)
    resident = lambda arr: pl.BlockSpec(arr.shape, const, pipeline_mode=pl.Buffered(1))
    return pl.pallas_call(
        _ffn_body,
        grid=(n // tm,),
        in_specs=[
            pl.BlockSpec((tm, d), row),
            pl.BlockSpec((tm, a.shape[1]), row),
            pl.BlockSpec((tm, o.shape[1]), row),
            resident(wo), resident(wg), resident(wu), resident(wd),
            pl.BlockSpec((1, d), const), pl.BlockSpec((1, d), const), pl.BlockSpec((1, d), const),
        ],
        out_specs=pl.BlockSpec((tm, d), row),
        out_shape=jax.ShapeDtypeStruct((n, d), x.dtype),
        compiler_params=pltpu.CompilerParams(
            dimension_semantics=("arbitrary",), vmem_limit_bytes=VMEM_LIMIT),
        name="ffn",
    )(x, a, o, wo, wg, wu, wd, g1, g2, g3)


def _rope_tables(pos):
    half = ROT_DIM // 2
    inv_freq = jnp.exp(-jnp.log(jnp.asarray(ROPE_THETA, F32)) * jnp.arange(half, dtype=F32) * (2.0 / ROT_DIM))
    d = np.arange(LANES) % HEAD_DIM
    rot = d < ROT_DIM
    freq = jnp.where(rot, inv_freq[d % half], 0.0)
    sign = jnp.asarray(np.where(d < half, -1.0, np.where(rot, 1.0, 0.0)), F32)
    ang = pos[:, None] * freq[None, :]
    sin = jnp.sin(ang)
    return jnp.cos(ang), sin, sin * sign[None, :]


def _pick_tile(n, want):
    t = min(n, want)
    while n % t:
        t //= 2
    return t


def kernel(x_prompt, x_sample, cache_k_win, cache_v_win, state_hgrn, w_in, w_out, w_gate, w_up, w_down,
           norm_pre_mix, norm_post_mix, norm_pre_ffn, norm_post_ffn, attn_sinks, rec_lb, rec_out_norm):
    depth = w_in.shape[0]
    assert depth == 1, "single-layer step"
    bsz, seq, d = x_prompt.shape
    nb, dec_seq, _ = x_sample.shape
    assert dec_seq == 1 and seq % WINDOW == 0
    w_keep = cache_k_win.shape[2]
    assert w_keep == WINDOW

    w_in_b = w_in[0].astype(BF16)
    w_out_b = w_out[0].astype(BF16)
    w_gate_b = w_gate[0].astype(BF16)
    w_up_b = w_up[0].astype(BF16)
    w_down_b = w_down[0].astype(BF16)
    g_pre, g_post, g_pre_f, g_post_f = norm_pre_mix, norm_post_mix, norm_pre_ffn, norm_post_ffn
    sinks = attn_sinks[0].astype(F32)
    nrm = rec_out_norm[0].astype(F32)

    tm = _pick_tile(seq, 512)
    tm_in = _pick_tile(seq, 1024)
    row_tabs = _rope_tables(jnp.arange(tm_in, dtype=F32))
    base_tabs = [t[:, None, :] for t in _rope_tables(jnp.arange(0, seq, tm_in, dtype=F32))]
    xp = x_prompt.reshape(bsz * seq, d)
    q, k, v, rq, lf, rk, ri, rg = _inproj(xp, g_pre, w_in_b, row_tabs, base_tabs, rec_lb,
                                          tm=tm_in, act_dtype=BF16)
    r3 = lambda z: z.reshape(bsz, seq, z.shape[-1])
    k3, v3 = r3(k), r3(v)
    yp, s_fin = _mixffn(xp, q, k, v, rq, lf, rk, ri, rg, sinks, nrm, w_out_b, w_gate_b, w_up_b, w_down_b,
                        g_post, g_pre_f, g_post_f, bsz=bsz, tm=tm, chunk=128)
    y_prompt = yp.reshape(bsz, seq, d)
    keep = min(WINDOW, seq)
    new_k_p = k3[:, seq - keep:].reshape(1, bsz, keep, N_KV_HEADS, HEAD_DIM).astype(cache_k_win.dtype)
    new_v_p = v3[:, seq - keep:].reshape(1, bsz, keep, N_KV_HEADS, HEAD_DIM).astype(cache_v_win.dtype)
    new_s_p = s_fin[None].astype(state_hgrn.dtype)

    row_tabs = _rope_tables(jnp.full((nb,), float(PAST_LEN), F32))
    base_tabs = [t[:, None, :] for t in _rope_tables(jnp.zeros((1,), F32))]
    xs = x_sample.reshape(nb, d)
    qs, ks, vs, rqs, _, rks, ris, rgs = _inproj(xs, g_pre, w_in_b, row_tabs, base_tabs, rec_lb,
                                                tm=nb, act_dtype=F32)
    gb = 8
    ck = cache_k_win[0].reshape(nb, w_keep, KV_WIDTH)
    cv = cache_v_win[0].reshape(nb, w_keep, KV_WIDTH)
    a_s, nk, nv = _dec_attn(qs, ks, vs, ck, cv, sinks, gb=gb)
    o_s, s_new = _dec_hgrn(rqs, rks, ris, rgs, nrm, state_hgrn[0], gb=gb)
    ys = _ffn(xs, a_s, o_s, w_out_b, w_gate_b, w_up_b, w_down_b, g_post, g_pre_f, g_post_f, tm=nb)
    y_sample = ys.reshape(nb, 1, d)
    new_k_s = nk.reshape(1, nb, w_keep, N_KV_HEADS, HEAD_DIM)
    new_v_s = nv.reshape(1, nb, w_keep, N_KV_HEADS, HEAD_DIM)
    new_s_s = s_new[None]

    return (y_prompt, y_sample, new_k_p, new_v_p, new_s_p, new_k_s, new_v_s, new_s_s)
```

```python
import functools

import jax
import jax.numpy as jnp
import numpy as np
from jax import lax
from jax.experimental import pallas as pl
from jax.experimental.pallas import tpu as pltpu

F32 = jnp.float32
BF16 = jnp.bfloat16

PAST_LEN = 16384
WINDOW = 128
HEAD_DIM = 64
N_Q_HEADS = 8
N_KV_HEADS = 2
ROT_DIM = HEAD_DIM // 4
ROPE_THETA = 500000.0
N_REC_HEADS = 4
REC_DK = 128
EPS = 1e-6

ATTN_WIDTH = N_Q_HEADS * HEAD_DIM
KV_WIDTH = N_KV_HEADS * HEAD_DIM
REC_WIDTH = N_REC_HEADS * REC_DK
LANES = 128
NEG = -1e30

VMEM_LIMIT = 56 * 1024 * 1024


def _dot(a, b):
    return jnp.dot(a, b, preferred_element_type=F32)


def _dot_nt(a, b):
    return lax.dot_general(a, b, (((1,), (1,)), ((), ())), preferred_element_type=F32)


def _dot_tn(a, b):
    return lax.dot_general(a, b, (((0,), (0,)), ((), ())), preferred_element_type=F32)


def _sigmoid(x):
    return 1.0 / (1.0 + jnp.exp(-x))


def _rms(x, g):
    return x * lax.rsqrt(jnp.mean(x * x, axis=-1, keepdims=True) + EPS) * g


def _inproj_steps(tm):
    return tm // NORM_ROWS + 6


def _inproj_tile(x_ref, g_ref, w_ref, cr_ref, sr_ref, srs_ref, cb_ref, sb_ref, sbs_ref, lbp_ref, h_ref,
                 q_ref, k_ref, v_ref, rq_ref, lf_ref, rk_ref, ri_ref, rg_ref):
    tm = x_ref.shape[0]
    for r0 in range(0, tm, NORM_ROWS):
        rows = slice(r0, r0 + NORM_ROWS)
        h_ref[rows, :] = _rms(x_ref[rows, :], g_ref[...]).astype(BF16)
        yield
    cr = cr_ref[...]
    cos = cr * cb_ref[...] - sr_ref[...] * sb_ref[...]
    sin = srs_ref[...] * cb_ref[...] + cr * sbs_ref[...]
    lane = lax.broadcasted_iota(jnp.int32, cos.shape, 1)
    first = (lane & (HEAD_DIM - 1)) < (ROT_DIM // 2)

    def rope(z):
        partner = jnp.where(first, pltpu.roll(z, LANES - ROT_DIM // 2, 1), pltpu.roll(z, ROT_DIM // 2, 1))
        return z * cos + partner * sin

    def proj(lo, width):
        return _dot(h_ref[...], w_ref[:, lo:lo + width])

    scale = HEAD_DIM ** -0.5
    zq = proj(0, ATTN_WIDTH)
    for j in range(ATTN_WIDTH // LANES):
        z = rope(zq[:, j * LANES:(j + 1) * LANES])
        q_ref[:, j * LANES:(j + 1) * LANES] = (z * scale).astype(q_ref.dtype)
    yield
    off = ATTN_WIDTH
    zkv = proj(off, 2 * KV_WIDTH)
    k_ref[...] = rope(zkv[:, :KV_WIDTH])
    v_ref[...] = zkv[:, KV_WIDTH:]
    off += 2 * KV_WIDTH
    yield

    z = proj(off, REC_WIDTH)
    rq_ref[...] = (z * _sigmoid(z)).astype(rq_ref.dtype)
    off += REC_WIDTH
    yield

    r = lbp_ref[...]
    e = jnp.exp(r - jnp.max(r, axis=0, keepdims=True))
    lb = e[0:1, :] / jnp.sum(e, axis=0, keepdims=True)
    z = proj(off, REC_WIDTH)
    f = lb + (1.0 - lb) * _sigmoid(z)
    lf_ref[...] = jnp.log(f)
    rk_ref[...] = (1.0 - f).astype(rk_ref.dtype)
    off += REC_WIDTH
    yield

    ri_ref[...] = proj(off, REC_WIDTH).astype(ri_ref.dtype)
    off += REC_WIDTH
    yield
    z = proj(off, REC_WIDTH)
    rg_ref[...] = (z * _sigmoid(z)).astype(rg_ref.dtype)
    yield


def _inproj_body(*refs):
    ins, outs, h_ref = refs[:10], refs[10:18], refs[18]
    for _ in _inproj_tile(*ins, h_ref, *outs):
        pass


def _inproj(x, g, w_bf, row_tabs, base_tabs, rec_lb, *, tm, act_dtype):
    n, d = x.shape
    n_base = base_tabs[0].shape[0]
    in_w = w_bf.shape[1]
    row = lambda i: (i, 0)
    const = lambda i: (0, 0)
    base = lambda i: (i % n_base, 0, 0)
    outs = [
        jax.ShapeDtypeStruct((n, ATTN_WIDTH), act_dtype),
        jax.ShapeDtypeStruct((n, KV_WIDTH), F32),
        jax.ShapeDtypeStruct((n, KV_WIDTH), F32),
        jax.ShapeDtypeStruct((n, REC_WIDTH), act_dtype),
        jax.ShapeDtypeStruct((n, REC_WIDTH), F32),
        jax.ShapeDtypeStruct((n, REC_WIDTH), act_dtype),
        jax.ShapeDtypeStruct((n, REC_WIDTH), act_dtype),
        jax.ShapeDtypeStruct((n, REC_WIDTH), act_dtype),
    ]
    return pl.pallas_call(
        _inproj_body,
        grid=(n // tm,),
        in_specs=[
            pl.BlockSpec((tm, d), row),
            pl.BlockSpec((1, d), const),
            pl.BlockSpec((d, in_w), const, pipeline_mode=pl.Buffered(1)),
            pl.BlockSpec((tm, LANES), const), pl.BlockSpec((tm, LANES), const), pl.BlockSpec((tm, LANES), const),
            pl.BlockSpec((None, 1, LANES), base), pl.BlockSpec((None, 1, LANES), base),
            pl.BlockSpec((None, 1, LANES), base),
            pl.BlockSpec(rec_lb.shape, const),
        ],
        out_specs=[pl.BlockSpec((tm, o.shape[1]), row) for o in outs],
        out_shape=outs,
        scratch_shapes=[pltpu.VMEM((tm, d), BF16)],
        compiler_params=pltpu.CompilerParams(
            dimension_semantics=("arbitrary",), vmem_limit_bytes=VMEM_LIMIT),
        name="inproj",
    )(x, g, w_bf, *row_tabs, *base_tabs, rec_lb)


def _head_variants(z):
    lane = lax.broadcasted_iota(jnp.int32, z.shape, 1)
    low = lane < HEAD_DIM
    z0 = jnp.where(low, z, 0.0)
    z1 = jnp.where(low, 0.0, z)
    return {
        (0, 0): z0.astype(BF16),
        (1, 1): z1.astype(BF16),
        (0, 1): pltpu.roll(z0, HEAD_DIM, 1).astype(BF16),
        (1, 0): pltpu.roll(z1, HEAD_DIM, 1).astype(BF16),
    }


def _swa_steps(nblk):
    return (nblk + 1) + nblk * N_KV_HEADS * (1 + (N_Q_HEADS // N_KV_HEADS // 2) * 3)


def _swa_tile(sink_ref, q_ref, kc_ref, vc_ref, kp_ref, vp_ref, a_ref, has_prev, nblk):
    w = WINDOW
    key = lax.broadcasted_iota(jnp.int32, (w, w), 0)
    qry = lax.broadcasted_iota(jnp.int32, (w, w), 1)
    cur_ok = key <= qry
    prev_band = key > qry

    kvar, vt = [], []
    for j in range(-1, nblk):
        kblk = kp_ref[...] if j < 0 else kc_ref[j * w:(j + 1) * w, :]
        vblk = vp_ref[...] if j < 0 else vc_ref[j * w:(j + 1) * w, :]
        kvar.append(_head_variants(kblk))
        vt.append(vblk.T.astype(BF16))
        yield

    group = N_Q_HEADS // N_KV_HEADS
    for j in range(nblk):
        prev_ok = prev_band if j > 0 else jnp.logical_and(prev_band, has_prev)
        for h in range(N_KV_HEADS):
            lhs = jnp.concatenate([kvar[j][(h, 0)], kvar[j + 1][(h, 0)],
                                   kvar[j][(h, 1)], kvar[j + 1][(h, 1)]], axis=0)
            vth = jnp.concatenate([vt[j][h * HEAD_DIM:(h + 1) * HEAD_DIM, :],
                                   vt[j + 1][h * HEAD_DIM:(h + 1) * HEAD_DIM, :]], axis=1)
            c0 = h * (group // 2)
            q2 = jnp.concatenate([q_ref[j * w:(j + 1) * w, (c0 + cc) * LANES:(c0 + cc + 1) * LANES]
                                  for cc in range(group // 2)], axis=0)
            st = _dot_nt(lhs, q2)
            yield
            for cc in range(group // 2):
                c = c0 + cc
                qcols = slice(cc * w, (cc + 1) * w)
                ps, rinv = [], []
                for half in range(2):
                    sink = sink_ref[2 * c + half]
                    base = half * 2 * w
                    sp = jnp.where(prev_ok, st[base:base + w, qcols], NEG)
                    sc = jnp.where(cur_ok, st[base + w:base + 2 * w, qcols], NEG)
                    m = jnp.maximum(jnp.max(sp, axis=0, keepdims=True), jnp.max(sc, axis=0, keepdims=True))
                    m = jnp.maximum(m, sink)
                    pp = jnp.exp(sp - m)
                    pc = jnp.exp(sc - m)
                    den = (jnp.sum(pp, axis=0, keepdims=True) + jnp.sum(pc, axis=0, keepdims=True)
                           + jnp.exp(sink - m))
                    ps.append(jnp.concatenate([pp, pc], axis=0).astype(BF16))
                    rinv.append(1.0 / den)
                    yield
                ot2 = _dot(vth, jnp.concatenate(ps, axis=1))
                ot = jnp.concatenate([ot2[:, :w] * rinv[0], ot2[:, w:] * rinv[1]], axis=0)
                a_ref[j * w:(j + 1) * w, c * LANES:(c + 1) * LANES] = ot.T.astype(a_ref.dtype)
                yield


def _split3(x):
    hi = x.astype(BF16)
    r1 = x - hi.astype(F32)
    mid = r1.astype(BF16)
    lo = (r1 - mid.astype(F32)).astype(BF16)
    return hi, mid, lo


def _level_table(c):
    t = np.arange(c)[:, None]
    s = np.arange(c)[None, :]
    x = t ^ s
    lvl = np.where(x > 0, np.floor(np.log2(np.maximum(x, 1))).astype(np.int32), 0)
    nlev = int(np.log2(c))
    return np.where(t > s, lvl, np.where(t == s, nlev, -1)).astype(np.int32)


HEAD_PAIR = 2 * REC_DK


def _hgrn_steps(chunk, nchunk):
    return nchunk * (1 + (N_REC_HEADS // 2) * (chunk // 32 + 3))


def _block_diag(a):
    z = jnp.zeros_like(a[:, :REC_DK])
    return jnp.concatenate([jnp.concatenate([a[:, :REC_DK], z], axis=1),
                            jnp.concatenate([z, a[:, REC_DK:]], axis=1)], axis=0)


def _hgrn_tile(q_ref, lf_ref, k_ref, v_ref, g_ref, nrm_ref, lvl_ref, o_ref,
               st_ref, b_ref, q32_ref, k32_ref, chunk, nchunk):
    c = chunk
    sub = 8
    blk = 16
    nlev = c.bit_length() - 1
    first_slab = 4
    row = lax.broadcasted_iota(jnp.int32, (c, c), 0)
    col = lax.broadcasted_iota(jnp.int32, (c, c), 1)
    tri = (col <= row).astype(BF16)
    tri2 = jnp.concatenate([tri, tri], axis=1)
    brow_i = lax.broadcasted_iota(jnp.int32, (blk, HEAD_PAIR), 0)
    odd = (brow_i & 1) != 0
    side1 = (brow_i & 2) != 0
    side2 = (brow_i & 4) != 0
    side3 = (brow_i & 8) != 0
    low4 = lax.broadcasted_iota(jnp.int32, (sub, HEAD_PAIR), 0) < 4
    lane = lax.broadcasted_iota(jnp.int32, (sub, c), 1)
    cat = lambda parts: parts[0] if len(parts) == 1 else jnp.concatenate(parts, axis=0)

    for ci in range(nchunk):
        sl = slice(ci * c, (ci + 1) * c)
        hi, mid, lo = _split3(lf_ref[sl, :])
        b_ref[...] = _dot(tri2, jnp.concatenate([hi, mid], axis=0)) + _dot(tri, lo)
        q32_ref[...] = q_ref[sl, :].astype(F32)
        k32_ref[...] = k_ref[sl, :].astype(F32)
        lvl = lvl_ref[...]
        yield

        for hp in range(N_REC_HEADS // 2):
            ps = slice(hp * HEAD_PAIR, (hp + 1) * HEAD_PAIR)

            def bcast(r, n=sub):
                return jnp.broadcast_to(b_ref[r:r + 1, ps], (n, HEAD_PAIR))

            blast = b_ref[c - 1:c, ps]
            q_in, k_out = [], []
            lev = [[] for _ in range(nlev)]
            lev_q = [[] for _ in range(nlev)]
            for r0 in range(0, c, blk):
                q = q32_ref[r0:r0 + blk, ps]
                k = k32_ref[r0:r0 + blk, ps]
                b = b_ref[r0:r0 + blk, ps]
                lf = lf_ref[ci * c + r0:ci * c + r0 + blk, ps]
                q_in.append((q * jnp.exp(b)).astype(BF16))
                k_out.append((k * jnp.exp(blast - b)).astype(BF16))
                lev[0].append((jnp.where(odd, q, k) * jnp.exp(jnp.where(odd, lf, 0.0))).astype(BF16))
                bref = jnp.concatenate([jnp.where(low4, bcast(r0 + 1), bcast(r0 + 5)),
                                        jnp.where(low4, bcast(r0 + 9), bcast(r0 + 13))], axis=0)
                lev[1].append((jnp.where(side1, q, k) * jnp.exp(-jnp.abs(b - bref))).astype(BF16))
                bref = jnp.concatenate([bcast(r0 + 3), bcast(r0 + 11)], axis=0)
                lev[2].append((jnp.where(side2, q, k) * jnp.exp(-jnp.abs(b - bref))).astype(BF16))
                bref = bcast(r0 + 7, blk)
                lev[3].append((jnp.where(side3, q, k) * jnp.exp(-jnp.abs(b - bref))).astype(BF16))
                for l in range(first_slab, nlev):
                    m = 1 << l
                    g0 = (r0 // (2 * m)) * (2 * m)
                    bref = bcast(g0 + m - 1, blk)
                    if r0 - g0 < m:
                        lev[l].append((k * jnp.exp(bref - b)).astype(BF16))
                    else:
                        piece = (q * jnp.exp(b - bref)).astype(BF16)
                        lev[l].append(piece)
                        lev_q[l].append(piece)
                if r0 % (2 * blk) == blk:
                    yield
            q_in, k_out = cat(q_in), cat(k_out)

            halves = [slice(0, REC_DK), slice(REC_DK, HEAD_PAIR)]
            pd = _dot_nt(q_ref[sl, ps], _block_diag(k_ref[sl, ps]))
            sc = [jnp.where(lvl == nlev, pd[:, hv], 0.0) for hv in halves]
            for l in range(first_slab):
                a = cat(lev[l])
                p = _dot_nt(a, _block_diag(a))
                sc = [jnp.where(lvl == l, p[:, hv], s) for hv, s in zip(halves, sc)]
            yield
            blocks = [[s[r0:r0 + sub, :] for r0 in range(0, c, sub)] for s in sc]
            for l in range(first_slab, nlev):
                m = 1 << l
                pq = _dot_nt(cat(lev_q[l]), _block_diag(cat(lev[l])))
                for gi, g0 in enumerate(range(0, c, 2 * m)):
                    in_group = jnp.logical_and(lane >= g0, lane < g0 + m)
                    for r in range(0, m, sub):
                        bi = (g0 + m + r) // sub
                        for hh, hv in enumerate(halves):
                            blocks[hh][bi] = jnp.where(in_group, pq[gi * m + r:gi * m + r + sub, hv], blocks[hh][bi])
            sc2 = jnp.concatenate([jnp.concatenate(bl, axis=0) for bl in blocks], axis=1).astype(BF16)
            yield
            st2 = jnp.concatenate([st_ref[2 * hp], st_ref[2 * hp + 1]], axis=1)
            oh2 = (_dot_nt(q_in, _block_diag(st2.astype(BF16)))
                   + _dot(sc2, _block_diag(v_ref[sl, ps])))
            for hh, hv in enumerate(halves):
                h = 2 * hp + hh
                hs = slice(h * REC_DK, (h + 1) * REC_DK)
                st_ref[h] = st2[:, hv] * jnp.exp(b_ref[c - 1:c, hs]) + _dot_tn(v_ref[sl, hs], k_out[:, hv])
                oh = oh2[:, hv]
                oh = oh * lax.rsqrt(jnp.mean(oh * oh, axis=-1, keepdims=True) + EPS) * nrm_ref[h:h + 1, :]
                o_ref[sl, hs] = (oh * g_ref[sl, hs].astype(F32)).astype(o_ref.dtype)
            yield


def _ffn_rows(x_ref, a_ref, o_ref, wo_ref, wg_ref, wu_ref, wd_ref, g1_ref, g2_ref, g3_ref, y_ref, rows):
    aw = a_ref.shape[1]
    mix = _dot(a_ref[rows, :], wo_ref[0:aw, :]) + _dot(o_ref[rows, :], wo_ref[aw:, :])
    x1 = x_ref[rows, :] + _rms(mix, g1_ref[...])
    h2 = _rms(x1, g2_ref[...]).astype(BF16)
    gate = _dot(h2, wg_ref[...])
    up = _dot(h2, wu_ref[...])
    act = (gate * _sigmoid(gate) * up).astype(BF16)
    ffn = _dot(act, wd_ref[...])
    y_ref[rows, :] = x1 + _rms(ffn, g3_ref[...])


FFN_COLS = 256
FFN_ROWS = 256
NORM_ROWS = 128
INPROJ_WEIGHT = 4


def _post_steps(tm):
    return (tm // FFN_ROWS) * (1 + FFN_ROWS // NORM_ROWS)


def _post_tile(x_ref, mix_ref, wo_ref, g1_ref, g2_ref, x1_ref, h2_ref):
    tm = x_ref.shape[0]
    for r0 in range(0, tm, FFN_ROWS):
        mix = _dot(mix_ref[r0:r0 + FFN_ROWS, :], wo_ref[...])
        yield
        for r1 in range(0, FFN_ROWS, NORM_ROWS):
            rows = slice(r0 + r1, r0 + r1 + NORM_ROWS)
            x1 = x_ref[rows, :] + _rms(mix[r1:r1 + NORM_ROWS, :], g1_ref[...])
            x1_ref[rows, :] = x1
            h2_ref[rows, :] = _rms(x1, g2_ref[...]).astype(BF16)
            yield


def _ffn_steps(tm, d, d_ff):
    return (tm // FFN_ROWS) * (d_ff // FFN_COLS + d // FFN_COLS) + tm // NORM_ROWS


def _ffn_tile(wg_ref, wu_ref, wd_ref, g3_ref, y_ref, x1_ref, h2_ref, act_ref, acc_ref):
    tm, d = x1_ref.shape
    d_ff = wg_ref.shape[1]
    for c0 in range(0, d_ff, FFN_COLS):
        cols = slice(c0, c0 + FFN_COLS)
        for r0 in range(0, tm, FFN_ROWS):
            rows = slice(r0, r0 + FFN_ROWS)
            gate = _dot(h2_ref[rows, :], wg_ref[:, cols])
            up = _dot(h2_ref[rows, :], wu_ref[:, cols])
            act_ref[rows, cols] = (gate * _sigmoid(gate) * up).astype(BF16)
            yield
    for c0 in range(0, d, FFN_COLS):
        cols = slice(c0, c0 + FFN_COLS)
        for r0 in range(0, tm, FFN_ROWS):
            rows = slice(r0, r0 + FFN_ROWS)
            acc_ref[rows, cols] = _dot(act_ref[rows, :], wd_ref[:, cols])
            yield
    for r0 in range(0, tm, NORM_ROWS):
        rows = slice(r0, r0 + NORM_ROWS)
        y_ref[rows, :] = x1_ref[rows, :] + _rms(acc_ref[rows, :], g3_ref[...])
        yield


def _merged(*streams):
    live = [[0, n, gen] for gen, n in streams]
    while live:
        cur = min(live, key=lambda e: e[0] / e[1])
        try:
            next(cur[2])
            cur[0] += 1
            yield
        except StopIteration:
            live.remove(cur)


def _chain(*gens):
    for gen in gens:
        yield from gen


def _mixffn_body(sink_ref, x_ref, gin_ref, win_ref, cr_ref, sr_ref, srs_ref, cb_ref, sb_ref, sbs_ref, lbp_ref,
                 nrm_ref, lvl_ref, wo_ref, wg_ref, wu_ref, wd_ref, g1_ref, g2_ref, g3_ref,
                 y_ref, sfin_ref, kwin_ref, vwin_ref,
                 h_scr, q_scr, k_scr, v_scr, kp_scr, vp_scr, rq_scr, lf_scr, rk_scr, ri_scr, rg_scr,
                 mix_scr, x1_scr, h2_scr, act_scr, acc_scr, st_ref, b_ref, q32_ref, k32_ref,
                 *, tiles_per_seq, ntiles, nblk, chunk, nchunk):
    g = pl.program_id(0)
    real = g < ntiles
    t_in = jnp.minimum(g, ntiles - 1) % tiles_per_seq
    cur = g % 2

    @pl.when(g == 0)
    def _():
        x1_scr[0] = jnp.zeros(x1_scr.shape[1:], x1_scr.dtype)
        h2_scr[0] = jnp.zeros(h2_scr.shape[1:], h2_scr.dtype)
        kp_scr[...] = jnp.zeros_like(kp_scr)
        vp_scr[...] = jnp.zeros_like(vp_scr)

    @pl.when(jnp.logical_and(real, t_in == 0))
    def _():
        st_ref[...] = jnp.zeros_like(st_ref)

    tm, d = x_ref.shape
    w = KV_WIDTH
    n_mix = _swa_steps(nblk) + _hgrn_steps(chunk, nchunk)
    inproj = _inproj_tile(x_ref, gin_ref, win_ref, cr_ref, sr_ref, srs_ref, cb_ref, sb_ref, sbs_ref, lbp_ref, h_scr,
                          q_scr, k_scr, v_scr, rq_scr, lf_scr, rk_scr, ri_scr, rg_scr)
    mixers = _merged(
        (_swa_tile(sink_ref, q_scr, k_scr, v_scr, kp_scr, vp_scr, mix_scr.at[:, 0:ATTN_WIDTH], t_in > 0, nblk),
         _swa_steps(nblk)),
        (_hgrn_tile(rq_scr, lf_scr, rk_scr, ri_scr, rg_scr, nrm_ref, lvl_ref, mix_scr.at[:, ATTN_WIDTH:],
                    st_ref, b_ref, q32_ref, k32_ref, chunk, nchunk), _hgrn_steps(chunk, nchunk)))
    post = _post_tile(x_ref, mix_scr, wo_ref, g1_ref, g2_ref, x1_scr.at[1 - cur], h2_scr.at[1 - cur])
    for _ in _merged(
            (_ffn_tile(wg_ref, wu_ref, wd_ref, g3_ref, y_ref, x1_scr.at[cur], h2_scr.at[cur], act_scr, acc_scr),
             _ffn_steps(tm, d, wg_ref.shape[1])),
            (_chain(inproj, mixers, post), INPROJ_WEIGHT * _inproj_steps(tm) + n_mix + _post_steps(tm))):
        pass
    kp_scr[...] = k_scr[tm - w:, :]
    vp_scr[...] = v_scr[tm - w:, :]

    @pl.when(jnp.logical_and(real, t_in == tiles_per_seq - 1))
    def _():
        for h in range(N_REC_HEADS):
            sfin_ref[h] = st_ref[h].T
        kwin_ref[...] = k_scr[tm - w:, :]
        vwin_ref[...] = v_scr[tm - w:, :]


def _mixffn(x, sinks, nrm, rec_lb, g_in, w_in, row_tabs, base_tabs, wo, wg, wu, wd, g1, g2, g3, *, bsz, tm, chunk):
    n, d = x.shape
    ntiles = n // tm
    tiles_per_seq = ntiles // bsz
    nblk = tm // WINDOW
    w = REC_WIDTH
    lvl = jnp.asarray(_level_table(chunk))
    tile = lambda g: jnp.minimum(g, ntiles - 1)
    mix = lambda g: (tile(g), 0)
    base = lambda g: (tile(g) % tiles_per_seq, 0, 0)
    seq = lambda g: (tile(g) // tiles_per_seq, 0, 0)
    ffn = lambda g: (jnp.maximum(g - 1, 0), 0)
    const = lambda g: (0, 0)
    resident = lambda arr: pl.BlockSpec(arr.shape, const, pipeline_mode=pl.Buffered(1))
    return pl.pallas_call(
        functools.partial(_mixffn_body, tiles_per_seq=tiles_per_seq, ntiles=ntiles, nblk=nblk,
                          chunk=chunk, nchunk=tm // chunk),
        grid=(ntiles + 1,),
        in_specs=[
            pl.BlockSpec(memory_space=pltpu.SMEM),
            pl.BlockSpec((tm, d), mix),
            pl.BlockSpec((1, d), const),
            resident(w_in),
            resident(row_tabs[0]), resident(row_tabs[1]), resident(row_tabs[2]),
            pl.BlockSpec((None, 1, LANES), base), pl.BlockSpec((None, 1, LANES), base),
            pl.BlockSpec((None, 1, LANES), base),
            pl.BlockSpec(rec_lb.shape, const),
            pl.BlockSpec(nrm.shape, const),
            pl.BlockSpec((chunk, chunk), const),
            resident(wo), resident(wg), resident(wu), resident(wd),
            pl.BlockSpec((1, d), const), pl.BlockSpec((1, d), const), pl.BlockSpec((1, d), const),
        ],
        out_specs=[
            pl.BlockSpec((tm, d), ffn),
            pl.BlockSpec((None, N_REC_HEADS, REC_DK, REC_DK), lambda g: seq(g) + (0,)),
            pl.BlockSpec((None, WINDOW, KV_WIDTH), seq),
            pl.BlockSpec((None, WINDOW, KV_WIDTH), seq),
        ],
        out_shape=[
            jax.ShapeDtypeStruct((n, d), x.dtype),
            jax.ShapeDtypeStruct((bsz, N_REC_HEADS, REC_DK, REC_DK), F32),
            jax.ShapeDtypeStruct((bsz, WINDOW, KV_WIDTH), F32),
            jax.ShapeDtypeStruct((bsz, WINDOW, KV_WIDTH), F32),
        ],
        scratch_shapes=[
            pltpu.VMEM((tm, d), BF16),
            pltpu.VMEM((tm, ATTN_WIDTH), BF16),
            pltpu.VMEM((tm, KV_WIDTH), F32),
            pltpu.VMEM((tm, KV_WIDTH), F32),
            pltpu.VMEM((WINDOW, KV_WIDTH), F32),
            pltpu.VMEM((WINDOW, KV_WIDTH), F32),
            pltpu.VMEM((tm, w), BF16),
            pltpu.VMEM((tm, w), F32),
            pltpu.VMEM((tm, w), BF16),
            pltpu.VMEM((tm, w), BF16),
            pltpu.VMEM((tm, w), BF16),
            pltpu.VMEM((tm, ATTN_WIDTH + w), BF16),
            pltpu.VMEM((2, tm, d), F32),
            pltpu.VMEM((2, tm, d), BF16),
            pltpu.VMEM((tm, wg.shape[1]), BF16),
            pltpu.VMEM((tm, d), F32),
            pltpu.VMEM((N_REC_HEADS, REC_DK, REC_DK), F32),
            pltpu.VMEM((chunk, w), F32),
            pltpu.VMEM((chunk, w), F32),
            pltpu.VMEM((chunk, w), F32),
        ],
        compiler_params=pltpu.CompilerParams(
            dimension_semantics=("arbitrary",), vmem_limit_bytes=VMEM_LIMIT),
        name="mixffn",
    )(sinks, x, g_in, w_in, *row_tabs, *base_tabs, rec_lb, nrm, lvl, wo, wg, wu, wd, g1, g2, g3)


def _dec_attn_body(sink_ref, q_ref, kn_ref, vn_ref, ck_ref, cv_ref, a_ref, nk_ref, nv_ref,
                   qs_ref, os_ref, *, gb):
    group = N_Q_HEADS // N_KV_HEADS
    lane = lax.broadcasted_iota(jnp.int32, (gb, LANES), 1)
    low = lane < HEAD_DIM
    q = q_ref[...]
    for hq in range(N_Q_HEADS):
        z = q[:, (hq // 2) * LANES:(hq // 2 + 1) * LANES]
        z = jnp.where(low, z, 0.0) if hq % 2 == 0 else jnp.where(low, 0.0, z)
        if hq % 2 != hq // group:
            z = pltpu.roll(z, HEAD_DIM, 1)
        qs_ref[hq * gb:(hq + 1) * gb, :] = z

    hrow = lax.broadcasted_iota(jnp.int32, (N_Q_HEADS, 1), 0)
    sink = jnp.zeros((N_Q_HEADS, 1), F32)
    for hq in range(N_Q_HEADS):
        sink = jnp.where(hrow == hq, sink_ref[hq], sink)
    pad = jnp.zeros((8, LANES), BF16)
    w = ck_ref.shape[1]
    last = lax.broadcasted_iota(jnp.int32, (w, LANES), 0) == w - 1

    scores = []
    for j in range(gb):
        kk = jnp.where(last, kn_ref[j:j + 1, :], pltpu.roll(ck_ref[j], w - 1, 0))
        nk_ref[j] = kk
        qp = qs_ref[pl.ds(j, N_Q_HEADS, stride=gb), :]
        qp = jnp.concatenate([qp.astype(BF16), pad], axis=0)
        scores.append(_dot_nt(qp, kk.astype(BF16))[0:N_Q_HEADS, :])
    probs, dens = [], []
    for s in scores:
        m = jnp.maximum(jnp.max(s, axis=-1, keepdims=True), sink)
        p = jnp.exp(s - m)
        dens.append(jnp.sum(p, axis=-1, keepdims=True) + jnp.exp(sink - m))
        probs.append(jnp.concatenate([p.astype(BF16), pad], axis=0))
    for j in range(gb):
        vv = jnp.where(last, vn_ref[j:j + 1, :], pltpu.roll(cv_ref[j], w - 1, 0))
        nv_ref[j] = vv
        o = _dot(probs[j], vv.astype(BF16))[0:N_Q_HEADS, :] / dens[j]
        os_ref[j * N_Q_HEADS:(j + 1) * N_Q_HEADS, :] = o

    for cidx in range(ATTN_WIDTH // LANES):
        parts = []
        for half in range(2):
            hq = 2 * cidx + half
            z = os_ref[pl.ds(hq, gb, stride=N_Q_HEADS), :]
            if half != hq // group:
                z = pltpu.roll(z, HEAD_DIM, 1)
            parts.append(z)
        a_ref[:, cidx * LANES:(cidx + 1) * LANES] = jnp.where(low, parts[0], parts[1]).astype(a_ref.dtype)


def _dec_attn(q, k_new, v_new, cache_k, cache_v, sinks, *, gb):
    nb = q.shape[0]
    w = cache_k.shape[1]
    row = lambda i: (i, 0)
    blk3 = lambda i: (i, 0, 0)
    return pl.pallas_call(
        functools.partial(_dec_attn_body, gb=gb),
        grid=(nb // gb,),
        in_specs=[
            pl.BlockSpec(memory_space=pltpu.SMEM),
            pl.BlockSpec((gb, ATTN_WIDTH), row),
            pl.BlockSpec((gb, KV_WIDTH), row),
            pl.BlockSpec((gb, KV_WIDTH), row),
            pl.BlockSpec((gb, w, KV_WIDTH), blk3),
            pl.BlockSpec((gb, w, KV_WIDTH), blk3),
        ],
        out_specs=[
            pl.BlockSpec((gb, ATTN_WIDTH), row),
            pl.BlockSpec((gb, w, KV_WIDTH), blk3),
            pl.BlockSpec((gb, w, KV_WIDTH), blk3),
        ],
        out_shape=[
            jax.ShapeDtypeStruct((nb, ATTN_WIDTH), BF16),
            jax.ShapeDtypeStruct(cache_k.shape, cache_k.dtype),
            jax.ShapeDtypeStruct(cache_v.shape, cache_v.dtype),
        ],
        scratch_shapes=[
            pltpu.VMEM((N_Q_HEADS * gb, LANES), F32),
            pltpu.VMEM((N_Q_HEADS * gb, LANES), F32),
        ],
        compiler_params=pltpu.CompilerParams(
            dimension_semantics=("arbitrary",), vmem_limit_bytes=VMEM_LIMIT),
        name="dec_attn",
    )(sinks, q, k_new, v_new, cache_k, cache_v)


def _dec_hgrn_body(q_ref, k_ref, i_ref, g_ref, nrm_ref, s_ref, o_ref, sn_ref, os_ref, *, gb):
    dk = REC_DK
    zpad = jnp.zeros((dk - gb, dk), F32)
    for h in range(N_REC_HEADS):
        hs = slice(h * dk, (h + 1) * dk)
        kt = jnp.concatenate([k_ref[:, hs], zpad], axis=0).T
        for j in range(gb):
            kcol = kt[:, j:j + 1]
            sn_ref[j, h] = s_ref[j, h] * (1.0 - kcol) + kcol * i_ref[j:j + 1, hs]
        for j in range(gb):
            qrow = jnp.broadcast_to(q_ref[j:j + 1, hs], (16, dk)).astype(BF16)
            os_ref[j:j + 1, hs] = _dot(qrow, sn_ref[j, h].astype(BF16))[0:1, :]
    for h in range(N_REC_HEADS):
        hs = slice(h * dk, (h + 1) * dk)
        oh = os_ref[:, hs]
        oh = oh * lax.rsqrt(jnp.mean(oh * oh, axis=-1, keepdims=True) + EPS) * nrm_ref[h:h + 1, :]
        o_ref[:, hs] = (oh * g_ref[:, hs]).astype(o_ref.dtype)


def _dec_hgrn(q, k, iv, g, nrm, state, *, gb):
    nb, w = q.shape
    row = lambda i: (i, 0)
    blk4 = lambda i: (i, 0, 0, 0)
    return pl.pallas_call(
        functools.partial(_dec_hgrn_body, gb=gb),
        grid=(nb // gb,),
        in_specs=[
            pl.BlockSpec((gb, w), row),
            pl.BlockSpec((gb, w), row),
            pl.BlockSpec((gb, w), row),
            pl.BlockSpec((gb, w), row),
            pl.BlockSpec(nrm.shape, lambda i: (0, 0)),
            pl.BlockSpec((gb,) + state.shape[1:], blk4),
        ],
        out_specs=[
            pl.BlockSpec((gb, w), row),
            pl.BlockSpec((gb,) + state.shape[1:], blk4),
        ],
        out_shape=[
            jax.ShapeDtypeStruct((nb, w), BF16),
            jax.ShapeDtypeStruct(state.shape, state.dtype),
        ],
        scratch_shapes=[pltpu.VMEM((gb, w), F32)],
        compiler_params=pltpu.CompilerParams(
            dimension_semantics=("arbitrary",), vmem_limit_bytes=VMEM_LIMIT),
        name="dec_hgrn",
    )(q, k, iv, g, nrm, state)


def _ffn_body(x_ref, a_ref, o_ref, wo_ref, wg_ref, wu_ref, wd_ref, g1_ref, g2_ref, g3_ref, y_ref):
    _ffn_rows(x_ref, a_ref, o_ref, wo_ref, wg_ref, wu_ref, wd_ref, g1_ref, g2_ref, g3_ref, y_ref,
              slice(0, x_ref.shape[0]))


def _ffn(x, a, o, wo, wg, wu, wd, g1, g2, g3, *, tm):
    n, d = x.shape
    row = lambda i: (i, 0)
    const = lambda i: (0, 0)
    resident = lambda arr: pl.BlockSpec(arr.shape, const, pipeline_mode=pl.Buffered(1))
    return pl.pallas_call(
        _ffn_body,
        grid=(n // tm,),
        in_specs=[
            pl.BlockSpec((tm, d), row),
            pl.BlockSpec((tm, a.shape[1]), row),
            pl.BlockSpec((tm, o.shape[1]), row),
            resident(wo), resident(wg), resident(wu), resident(wd),
            pl.BlockSpec((1, d), const), pl.BlockSpec((1, d), const), pl.BlockSpec((1, d), const),
        ],
        out_specs=pl.BlockSpec((tm, d), row),
        out_shape=jax.ShapeDtypeStruct((n, d), x.dtype),
        compiler_params=pltpu.CompilerParams(
            dimension_semantics=("arbitrary",), vmem_limit_bytes=VMEM_LIMIT),
        name="ffn",
    )(x, a, o, wo, wg, wu, wd, g1, g2, g3)


def _rope_tables(pos):
    half = ROT_DIM // 2
    inv_freq = jnp.exp(-jnp.log(jnp.asarray(ROPE_THETA, F32)) * jnp.arange(half, dtype=F32) * (2.0 / ROT_DIM))
    d = np.arange(LANES) % HEAD_DIM
    rot = d < ROT_DIM
    freq = jnp.where(rot, inv_freq[d % half], 0.0)
    sign = jnp.asarray(np.where(d < half, -1.0, np.where(rot, 1.0, 0.0)), F32)
    ang = pos[:, None] * freq[None, :]
    sin = jnp.sin(ang)
    return jnp.cos(ang), sin, sin * sign[None, :]


def _pick_tile(n, want):
    t = min(n, want)
    while n % t:
        t //= 2
    return t


def kernel(x_prompt, x_sample, cache_k_win, cache_v_win, state_hgrn, w_in, w_out, w_gate, w_up, w_down,
           norm_pre_mix, norm_post_mix, norm_pre_ffn, norm_post_ffn, attn_sinks, rec_lb, rec_out_norm):
    depth = w_in.shape[0]
    assert depth == 1, "single-layer step"
    bsz, seq, d = x_prompt.shape
    nb, dec_seq, _ = x_sample.shape
    assert dec_seq == 1 and seq % WINDOW == 0
    w_keep = cache_k_win.shape[2]
    assert w_keep == WINDOW

    w_in_b = w_in[0].astype(BF16)
    w_out_b = w_out[0].astype(BF16)
    w_gate_b = w_gate[0].astype(BF16)
    w_up_b = w_up[0].astype(BF16)
    w_down_b = w_down[0].astype(BF16)
    g_pre, g_post, g_pre_f, g_post_f = norm_pre_mix, norm_post_mix, norm_pre_ffn, norm_post_ffn
    sinks = attn_sinks[0].astype(F32)
    nrm = rec_out_norm[0].astype(F32)

    tm = _pick_tile(seq, 512)
    row_tabs = _rope_tables(jnp.arange(tm, dtype=F32))
    base_tabs = [t[:, None, :] for t in _rope_tables(jnp.arange(0, seq, tm, dtype=F32))]
    xp = x_prompt.reshape(bsz * seq, d)
    yp, s_fin, k_win, v_win = _mixffn(xp, sinks, nrm, rec_lb, g_pre, w_in_b, row_tabs, base_tabs,
                                      w_out_b, w_gate_b, w_up_b, w_down_b, g_post, g_pre_f, g_post_f,
                                      bsz=bsz, tm=tm, chunk=128)
    y_prompt = yp.reshape(bsz, seq, d)
    assert seq >= WINDOW
    new_k_p = k_win.reshape(1, bsz, WINDOW, N_KV_HEADS, HEAD_DIM).astype(cache_k_win.dtype)
    new_v_p = v_win.reshape(1, bsz, WINDOW, N_KV_HEADS, HEAD_DIM).astype(cache_v_win.dtype)
    new_s_p = s_fin[None].astype(state_hgrn.dtype)

    row_tabs = _rope_tables(jnp.full((nb,), float(PAST_LEN), F32))
    base_tabs = [t[:, None, :] for t in _rope_tables(jnp.zeros((1,), F32))]
    xs = x_sample.reshape(nb, d)
    qs, ks, vs, rqs, _, rks, ris, rgs = _inproj(xs, g_pre, w_in_b, row_tabs, base_tabs, rec_lb,
                                                tm=nb, act_dtype=F32)
    gb = 8
    ck = cache_k_win[0].reshape(nb, w_keep, KV_WIDTH)
    cv = cache_v_win[0].reshape(nb, w_keep, KV_WIDTH)
    a_s, nk, nv = _dec_attn(qs, ks, vs, ck, cv, sinks, gb=gb)
    o_s, s_new = _dec_hgrn(rqs, rks, ris, rgs, nrm, state_hgrn[0], gb=gb)
    ys = _ffn(xs, a_s, o_s, w_out_b, w_gate_b, w_up_b, w_down_b, g_post, g_pre_f, g_post_f, tm=nb)
    y_sample = ys.reshape(nb, 1, d)
    new_k_s = nk.reshape(1, nb, w_keep, N_KV_HEADS, HEAD_DIM)
    new_v_s = nv.reshape(1, nb, w_keep, N_KV_HEADS, HEAD_DIM)
    new_s_s = s_new[None]

    return (y_prompt, y_sample, new_k_p, new_v_p, new_s_p, new_k_s, new_v_s, new_s_s)
```
